```python
import math
import jax, jax.numpy as jnp
from jax import lax
import numpy as np

D_MODEL = 1024
BATCH = 4
SEQ = 4096
DEPTH = 4

CHUNK = 64
N_EVEN = (DEPTH + 1) // 2
N_ODD = DEPTH // 2
EPS = 1e-6
GDN_HEAD_DIM = 128
GDN_WIDTH = D_MODEL // 2
GDN_HEADS = GDN_WIDTH // GDN_HEAD_DIM
CONV_K = 4
S5_WIDTH = D_MODEL - GDN_WIDTH
S5_GROUP = 16
S5_GROUPS = S5_WIDTH // S5_GROUP
S5_STATE = 64
EVEN_IN = 4 * GDN_WIDTH + 2 * GDN_HEADS + S5_WIDTH
DIFF_HEAD_DIM = 64
DIFF_HEADS = D_MODEL // (2 * DIFF_HEAD_DIM)
DIFF_WIDTH = DIFF_HEADS * 2 * DIFF_HEAD_DIM
Q_BLOCK = 128
D_FF = ((8 * D_MODEL // 3 + 127) // 128) * 128
N_EXPERTS = 8
TOP_K = 2

kernel_name = 'hybrid_gdn_s5_diffattn_moe_adaln'


def rmsnorm(x, w):
    x32 = x.astype(jnp.float32)
    y = x32 * lax.rsqrt(jnp.mean(x32 * x32, axis=-1, keepdims=True) + EPS)
    return (y * w.astype(jnp.float32)).astype(x.dtype)


def l2norm(x):
    return x * lax.rsqrt(jnp.sum(x * x, axis=-1, keepdims=True) + EPS)


def causal_conv(x, w):
    k = w.shape[0]
    return lax.conv_general_dilated(x, w[:, None, :].astype(x.dtype), window_strides=(1,), padding=[(k - 1, 0)], dimension_numbers=('NWC', 'WIO', 'NWC'), feature_group_count=w.shape[1])


def to_chunks(t):
    bn, seq, nh, d = t.shape
    return t.reshape(bn, seq // CHUNK, CHUNK, nh, d).transpose(1, 0, 3, 2, 4)


def gated_delta_rule(q, k, v, g, beta):
    bn, seq, nh, dk = q.shape
    dv = v.shape[-1]
    q, k, v = to_chunks(q), to_chunks(k), to_chunks(v)
    g = jnp.cumsum(to_chunks(g[..., None])[..., 0], axis=-1)
    beta = to_chunks(beta[..., None])[..., 0]
    idx = jnp.arange(CHUNK)
    causal = idx[:, None] >= idx[None, :]
    strict = idx[:, None] > idx[None, :]
    decay = jnp.exp(jnp.where(causal, g[..., :, None] - g[..., None, :], -jnp.inf))
    k_beta = k * beta[..., None]
    m = jnp.where(strict, jnp.einsum('nbhid,nbhjd->nbhij', k_beta, k) * decay, 0.0)
    rhs = jnp.concatenate([v * beta[..., None], k_beta * jnp.exp(g)[..., None]], axis=-1)
    sol = lax.linalg.triangular_solve(jnp.eye(CHUNK, dtype=m.dtype) + m, rhs, left_side=True, lower=True)
    u, w = sol[..., :dv], sol[..., dv:]
    qk = jnp.where(causal, jnp.einsum('nbhid,nbhjd->nbhij', q, k) * decay, 0.0)

    def step(state, inp):
        q_c, k_c, u_c, w_c, g_c, qk_c = inp
        v_new = u_c - jnp.einsum('bhcd,bhde->bhce', w_c, state)
        o_c = jnp.einsum('bhcd,bhde->bhce', q_c * jnp.exp(g_c)[..., None], state) + jnp.einsum('bhij,bhje->bhie', qk_c, v_new)
        g_last = g_c[..., -1:]
        state = state * jnp.exp(g_last)[..., None] + jnp.einsum('bhcd,bhce->bhde', k_c * jnp.exp(g_last - g_c)[..., None], v_new)
        return state, o_c

    state0 = jnp.zeros((bn, nh, dk, dv), jnp.float32)
    _, o = lax.scan(step, state0, (q, k, u, w, g, qk))
    return o.transpose(1, 0, 3, 2, 4).reshape(bn, seq, nh, dv)


def s5_glu(u, lam_re, lam_im, log_step, b_re, b_im, c_re, c_im, d_skip, glu_w, glu_b):
    bn, seq, _ = u.shape
    f32 = jnp.float32
    u32 = u.astype(f32).reshape(bn, seq, S5_GROUPS, S5_GROUP)
    lr = jnp.minimum(lam_re.astype(f32), -1e-4)
    li = lam_im.astype(f32)
    dt = jnp.exp(log_step.astype(f32))[:, None]
    mag = jnp.exp(lr * dt)
    ar, ai = mag * jnp.cos(li * dt), mag * jnp.sin(li * dt)
    nr, ni = ar - 1.0, ai
    den = lr * lr + li * li
    cr, ci = (nr * lr + ni * li) / den, (ni * lr - nr * li) / den
    b_re, b_im = b_re.astype(f32), b_im.astype(f32)
    bbr = cr[..., None] * b_re - ci[..., None] * b_im
    bbi = cr[..., None] * b_im + ci[..., None] * b_re
    bu_r = jnp.einsum('gnc,blgc->blgn', bbr, u32)
    bu_i = jnp.einsum('gnc,blgc->blgn', bbi, u32)
    a_r = jnp.broadcast_to(ar, bu_r.shape)
    a_i = jnp.broadcast_to(ai, bu_r.shape)

    def combine(e1, e2):
        a1r, a1i, b1r, b1i = e1
        a2r, a2i, b2r, b2i = e2
        return (a2r * a1r - a2i * a1i, a2r * a1i + a2i * a1r, a2r * b1r - a2i * b1i + b2r, a2r * b1i + a2i * b1r + b2i)

    _, _, xr, xi = lax.associative_scan(combine, (a_r, a_i, bu_r, bu_i), axis=1)
    y = (jnp.einsum('gcn,blgn->blgc', c_re.astype(f32), xr) - jnp.einsum('gcn,blgn->blgc', c_im.astype(f32), xi)
         + d_skip.astype(f32).reshape(S5_GROUPS, S5_GROUP) * u32)
    y = jax.nn.gelu(y.reshape(bn, seq, S5_WIDTH), approximate=False)
    ga, gb = jnp.split(y @ glu_w.astype(f32) + glu_b.astype(f32), 2, axis=-1)
    return (ga * jax.nn.sigmoid(gb)).astype(u.dtype)


def hybrid_mixer(h, w_in, conv_w, a_log, dt_bias, gdn_norm_w, lam_re, lam_im, log_step, b_re, b_im, c_re, c_im, d_skip, glu_w, glu_b, w_out):
    bn, seq, _ = h.shape
    f32 = jnp.float32
    proj = h @ w_in
    qkv, z, b, a, u = jnp.split(proj, [3 * GDN_WIDTH, 4 * GDN_WIDTH, 4 * GDN_WIDTH + GDN_HEADS, 4 * GDN_WIDTH + 2 * GDN_HEADS], axis=-1)
    qkv = jax.nn.silu(causal_conv(qkv, conv_w)).astype(f32)
    q, k, v = (t.reshape(bn, seq, GDN_HEADS, GDN_HEAD_DIM) for t in jnp.split(qkv, 3, axis=-1))
    beta = jax.nn.sigmoid(b.astype(f32))
    g = -jnp.exp(a_log.astype(f32)) * jax.nn.softplus(a.astype(f32) + dt_bias.astype(f32))
    o = gated_delta_rule(l2norm(q) * GDN_HEAD_DIM ** -0.5, l2norm(k), v, g, beta)
    o = rmsnorm(o, gdn_norm_w) * jax.nn.silu(z.astype(f32).reshape(bn, seq, GDN_HEADS, GDN_HEAD_DIM))
    y_a = o.reshape(bn, seq, GDN_WIDTH).astype(h.dtype)
    y_b = s5_glu(u, lam_re, lam_im, log_step, b_re, b_im, c_re, c_im, d_skip, glu_w, glu_b)
    return jnp.concatenate([y_a, y_b], axis=-1) @ w_out


def diff_attention(h, w_qkv, q_norm_w, k_norm_w, lq1, lk1, lq2, lk2, subln_w, w_out, lambda_init):
    bn, seq, _ = h.shape
    f32 = jnp.float32
    q, k, v = jnp.split(h @ w_qkv, 3, axis=-1)
    q = rmsnorm(q.reshape(bn, seq, DIFF_HEADS, 2, DIFF_HEAD_DIM), q_norm_w).astype(f32) * DIFF_HEAD_DIM ** -0.5
    k = rmsnorm(k.reshape(bn, seq, DIFF_HEADS, 2, DIFF_HEAD_DIM), k_norm_w).astype(f32)
    v = v.reshape(bn, seq, DIFF_HEADS, 2 * DIFF_HEAD_DIM).astype(f32)
    lam = (jnp.exp(jnp.sum(lq1.astype(f32) * lk1.astype(f32))) - jnp.exp(jnp.sum(lq2.astype(f32) * lk2.astype(f32))) + lambda_init)
    n_blk = seq // Q_BLOCK
    qb = q.reshape(bn, n_blk, Q_BLOCK, DIFF_HEADS, 2, DIFF_HEAD_DIM).transpose(1, 0, 2, 3, 4, 5)
    key_chunk = jnp.arange(seq) // CHUNK

    def block(args):
        q_blk, blk = args
        q_chunk = (blk * Q_BLOCK + jnp.arange(Q_BLOCK)) // CHUNK
        allowed = key_chunk[None, :] <= q_chunk[:, None]
        s = jnp.einsum('bqhtd,bkhtd->bhtqk', q_blk, k)
        p = jax.nn.softmax(jnp.where(allowed, s, -jnp.inf), axis=-1)
        p = p[:, :, 0] - lam * p[:, :, 1]
        return jnp.einsum('bhqk,bkhe->bqhe', p, v)

    o = lax.map(block, (qb, jnp.arange(n_blk)))
    o = o.transpose(1, 0, 2, 3, 4).reshape(bn, seq, DIFF_HEADS, 2 * DIFF_HEAD_DIM)
    o = rmsnorm(o, subln_w) * (1.0 - lambda_init)
    return o.reshape(bn, seq, DIFF_WIDTH).astype(h.dtype) @ w_out


def swiglu(h, w13, w2):
    a, b = jnp.split(h @ w13, 2, axis=-1)
    return (jax.nn.silu(a) * b) @ w2


def moe_swiglu(h, router_w, w13, w2):
    logits = (h @ router_w).astype(jnp.float32)
    top_v, top_i = lax.top_k(logits, TOP_K)
    gates = jax.nn.softmax(top_v, axis=-1)
    combine = jnp.sum(jax.nn.one_hot(top_i, N_EXPERTS, dtype=jnp.float32) * gates[..., None], axis=-2)
    out = jnp.zeros(h.shape, h.dtype)
    for e in range(N_EXPERTS):
        out = out + combine[..., e:e + 1].astype(h.dtype) * swiglu(h, w13[e], w2[e])
    return out


def setup_inputs(seed: int = 0) -> dict:
    key = jax.random.key(seed)
    keys = iter(jax.random.split(key, 64))
    f32 = jnp.float32
    D, F, NE, NO = D_MODEL, D_FF, N_EVEN, N_ODD

    def nrm(shape, scale):
        return jax.random.normal(next(keys), shape, f32) * scale

    def gain(shape):
        return 1.0 + nrm(shape, 0.01)

    def log_uniform(shape, lo, hi):
        return jax.random.uniform(next(keys), shape, f32, minval=math.log(lo), maxval=math.log(hi))

    dt = jnp.exp(log_uniform((NE, GDN_HEADS), 1e-3, 1e-1))
    return {
        'x': nrm((BATCH, SEQ, D), 1.0),
        'c': nrm((BATCH, D), 1.0),
        'ada_w': nrm((DEPTH, D, 6 * D), 0.5 * D ** -0.5),
        'ada_b': nrm((DEPTH, 6 * D), 0.01),
        'norm_mix_w': gain((DEPTH, D)),
        'norm_ffn_w': gain((DEPTH, D)),
        'even_w_in': nrm((NE, D, EVEN_IN), D ** -0.5),
        'even_conv_w': nrm((NE, CONV_K, 3 * GDN_WIDTH), CONV_K ** -0.5),
        'even_a_log': jnp.log(jax.random.uniform(next(keys), (NE, GDN_HEADS), f32, minval=1.0, maxval=16.0)),
        'even_dt_bias': dt + jnp.log(-jnp.expm1(-dt)),
        'even_gdn_norm_w': gain((NE, GDN_HEAD_DIM)),
        'even_lam_re': -0.5 + nrm((NE, S5_GROUPS, S5_STATE), 0.01),
        'even_lam_im': math.pi * jnp.arange(S5_STATE, dtype=f32) + nrm((NE, S5_GROUPS, S5_STATE), 0.01),
        'even_log_step': log_uniform((NE, S5_GROUPS), 1e-3, 1e-1),
        'even_b_re': nrm((NE, S5_GROUPS, S5_STATE, S5_GROUP), (2 * S5_GROUP) ** -0.5),
        'even_b_im': nrm((NE, S5_GROUPS, S5_STATE, S5_GROUP), (2 * S5_GROUP) ** -0.5),
        'even_c_re': nrm((NE, S5_GROUPS, S5_GROUP, S5_STATE), S5_STATE ** -0.5),
        'even_c_im': nrm((NE, S5_GROUPS, S5_GROUP, S5_STATE), S5_STATE ** -0.5),
        'even_d_skip': nrm((NE, S5_WIDTH), 1.0),
        'even_glu_w': nrm((NE, S5_WIDTH, 2 * S5_WIDTH), S5_WIDTH ** -0.5),
        'even_glu_b': nrm((NE, 2 * S5_WIDTH), 0.01),
        'even_w_out': nrm((NE, GDN_WIDTH + S5_WIDTH, D), (GDN_WIDTH + S5_WIDTH) ** -0.5),
        'even_ffn_w13': nrm((NE, D, 2 * F), D ** -0.5),
        'even_ffn_w2': nrm((NE, F, D), F ** -0.5),
        'odd_w_qkv': nrm((NO, D, 3 * DIFF_WIDTH), D ** -0.5),
        'odd_q_norm_w': gain((NO, DIFF_HEAD_DIM)),
        'odd_k_norm_w': gain((NO, DIFF_HEAD_DIM)),
        'odd_lambda_q1': nrm((NO, DIFF_HEAD_DIM), 0.1),
        'odd_lambda_k1': nrm((NO, DIFF_HEAD_DIM), 0.1),
        'odd_lambda_q2': nrm((NO, DIFF_HEAD_DIM), 0.1),
        'odd_lambda_k2': nrm((NO, DIFF_HEAD_DIM), 0.1),
        'odd_subln_w': gain((NO, 2 * DIFF_HEAD_DIM)),
        'odd_w_out': nrm((NO, DIFF_WIDTH, D), DIFF_WIDTH ** -0.5),
        'odd_router_w': nrm((NO, D, N_EXPERTS), D ** -0.5),
        'odd_expert_w13': nrm((NO, N_EXPERTS, D, 2 * F), D ** -0.5),
        'odd_expert_w2': nrm((NO, N_EXPERTS, F, D), F ** -0.5),
    }


def reference(x, c, ada_w, ada_b, norm_mix_w, norm_ffn_w,
              even_w_in, even_conv_w, even_a_log, even_dt_bias, even_gdn_norm_w,
              even_lam_re, even_lam_im, even_log_step, even_b_re, even_b_im, even_c_re, even_c_im,
              even_d_skip, even_glu_w, even_glu_b, even_w_out, even_ffn_w13, even_ffn_w2,
              odd_w_qkv, odd_q_norm_w, odd_k_norm_w, odd_lambda_q1, odd_lambda_k1, odd_lambda_q2, odd_lambda_k2,
              odd_subln_w, odd_w_out, odd_router_w, odd_expert_w13, odd_expert_w2):
    cond = jax.nn.silu(c)
    for layer in range(DEPTH):
        i = layer // 2
        mod = (cond @ ada_w[layer] + ada_b[layer])[:, None, :].astype(x.dtype)
        sh1, sc1, g1, sh2, sc2, g2 = jnp.split(mod, 6, axis=-1)
        h = rmsnorm(x, norm_mix_w[layer]) * (1.0 + sc1) + sh1
        if layer % 2 == 0:
            y = hybrid_mixer(h, even_w_in[i], even_conv_w[i], even_a_log[i], even_dt_bias[i], even_gdn_norm_w[i],
                             even_lam_re[i], even_lam_im[i], even_log_step[i], even_b_re[i], even_b_im[i],
                             even_c_re[i], even_c_im[i], even_d_skip[i], even_glu_w[i], even_glu_b[i], even_w_out[i])
        else:
            lambda_init = 0.8 - 0.6 * math.exp(-0.3 * layer)
            y = diff_attention(h, odd_w_qkv[i], odd_q_norm_w[i], odd_k_norm_w[i], odd_lambda_q1[i], odd_lambda_k1[i],
                               odd_lambda_q2[i], odd_lambda_k2[i], odd_subln_w[i], odd_w_out[i], lambda_init)
        x = x + g1 * y
        h = rmsnorm(x, norm_ffn_w[layer]) * (1.0 + sc2) + sh2
        if layer % 2 == 0:
            f = swiglu(h, even_ffn_w13[i], even_ffn_w2[i])
        else:
            f = moe_swiglu(h, odd_router_w[i], odd_expert_w13[i], odd_expert_w2[i])
        x = x + g2 * f
    return x
```

```python
import functools
import math

import jax
import jax.numpy as jnp
from jax import lax
from jax.experimental import pallas as pl
from jax.experimental.pallas import tpu as pltpu

F32 = jnp.float32
BF16 = jnp.bfloat16

D_MODEL = 1024
DEPTH = 4
CHUNK = 64
EPS = 1e-6
GDN_HEAD_DIM = 128
GDN_WIDTH = D_MODEL // 2
GDN_HEADS = GDN_WIDTH // GDN_HEAD_DIM
CONV_K = 4
QKV_W = 3 * GDN_WIDTH
S5_WIDTH = D_MODEL - GDN_WIDTH
S5_GROUP = 16
S5_GROUPS = S5_WIDTH // S5_GROUP
S5_STATE = 64
S5_Q = 16
DIFF_HEAD_DIM = 64
DIFF_HEADS = D_MODEL // (2 * DIFF_HEAD_DIM)
D_FF = ((8 * D_MODEL // 3 + 127) // 128) * 128
N_EXPERTS = 8
LANES = 128
EVEN_IN_COLS = QKV_W + GDN_WIDTH + S5_WIDTH + LANES
VMEM_LIMIT = 56 * 1024 * 1024


def _cparams(*sem):
    return pltpu.CompilerParams(dimension_semantics=sem, vmem_limit_bytes=VMEM_LIMIT)


def _mm(a, b):
    return jnp.dot(a.astype(BF16), b.astype(BF16), preferred_element_type=F32)


def _mm_nt(a, b):
    return lax.dot_general(a.astype(BF16), b.astype(BF16), (((1,), (1,)), ((), ())),
                           preferred_element_type=F32)


def _mm_tn(a, b):
    return lax.dot_general(a.astype(BF16), b.astype(BF16), (((0,), (0,)), ((), ())),
                           preferred_element_type=F32)


def _mm_hi(a, b):
    a_hi = a.astype(BF16)
    a_lo = (a - a_hi.astype(F32)).astype(BF16)
    b_hi = b.astype(BF16)
    b_lo = (b - b_hi.astype(F32)).astype(BF16)
    return (jnp.dot(a_hi, b_hi, preferred_element_type=F32)
            + (jnp.dot(a_hi, b_lo, preferred_element_type=F32) + jnp.dot(a_lo, b_hi, preferred_element_type=F32)))


def _split3(x):
    hi = x.astype(BF16)
    r = x - hi.astype(F32)
    mid = r.astype(BF16)
    lo = (r - mid.astype(F32)).astype(BF16)
    return hi, mid, lo


def _norm_mod(x, nw, sc, sh):
    ms = jnp.mean(x * x, axis=-1, keepdims=True)
    return (x * lax.rsqrt(ms + EPS) * nw) * (1.0 + sc) + sh


def _silu(x):
    return x * jax.nn.sigmoid(x)


def _row_tile(n, pref):
    t = min(pref, n)
    while n % t:
        t //= 2
    return t


def _adaln_kernel(c_ref, w_ref, b_ref, o_ref):
    c = c_ref[...]
    o_ref[0] = _mm(_silu(c), w_ref[0]) + b_ref[0]


def _adaln(c_pad, ada_w, ada_b):
    depth, d, n = ada_w.shape
    tn = 1536
    rows = c_pad.shape[0]
    return pl.pallas_call(
        _adaln_kernel,
        grid=(depth, n // tn),
        in_specs=[pl.BlockSpec((rows, d), lambda l, j: (0, 0)),
                  pl.BlockSpec((1, d, tn), lambda l, j: (l, 0, j)),
                  pl.BlockSpec((1, 1, tn), lambda l, j: (l, 0, j))],
        out_specs=pl.BlockSpec((1, rows, tn), lambda l, j: (l, 0, j)),
        out_shape=jax.ShapeDtypeStruct((depth, rows, n), F32),
        compiler_params=_cparams("parallel", "parallel"),
        name="adaln",
    )(c_pad, ada_w, ada_b.reshape(depth, 1, n))


def _mod_spec(tm, seq):
    return pl.BlockSpec((1, 1, D_MODEL), lambda i, *_: ((i * tm) // seq, 0, 0))


def _even_in_kernel(x_ref, xh_ref, nw_ref, sc_ref, sh_ref, w_ref, cw_ref, gp_ref,
                    q_ref, k_ref, v_ref, z_ref, u_ref, gb_ref, pre_scr, *, tm, seq):
    i = pl.program_id(0)
    nw = nw_ref[...]
    sc = sc_ref[0]
    sh = sh_ref[0]
    h = _norm_mod(x_ref[...], nw, sc, sh).astype(BF16)
    proj = jnp.dot(h, w_ref[...], preferred_element_type=F32)
    hh = _norm_mod(xh_ref[...], nw, sc, sh).astype(BF16)
    preh = jnp.dot(hh, w_ref[:, :QKV_W], preferred_element_type=F32)
    preh = jnp.where((i * tm) % seq == 0, 0.0, preh)
    pre = proj[:, :QKV_W]
    pre_scr[0:8, :] = preh
    pre_scr[8:8 + tm, :] = pre
    acc = pre * cw_ref[CONV_K - 1:CONV_K, :]
    for j in range(CONV_K - 1):
        acc = acc + pre_scr[pl.ds(8 - (CONV_K - 1) + j, tm), :] * cw_ref[j:j + 1, :]
    qkv = _silu(acc)
    for hd in range(GDN_HEADS):
        lo = hd * GDN_HEAD_DIM
        qh = qkv[:, lo:lo + GDN_HEAD_DIM]
        kh = qkv[:, GDN_WIDTH + lo:GDN_WIDTH + lo + GDN_HEAD_DIM]
        q_ref[:, lo:lo + GDN_HEAD_DIM] = (
            qh * lax.rsqrt(jnp.sum(qh * qh, axis=-1, keepdims=True) + EPS) * GDN_HEAD_DIM ** -0.5)
        k_ref[:, lo:lo + GDN_HEAD_DIM] = kh * lax.rsqrt(jnp.sum(kh * kh, axis=-1, keepdims=True) + EPS)
    v_ref[...] = qkv[:, 2 * GDN_WIDTH:]
    z_ref[...] = proj[:, QKV_W:QKV_W + GDN_WIDTH]
    u_ref[...] = proj[:, QKV_W + GDN_WIDTH:QKV_W + GDN_WIDTH + S5_WIDTH]
    ba = proj[:, QKV_W + GDN_WIDTH + S5_WIDTH:]
    lane = lax.broadcasted_iota(jnp.int32, ba.shape, 1)
    t = ba + gp_ref[1:2, :]
    softplus = jnp.maximum(t, 0.0) + jnp.log1p(jnp.exp(-jnp.abs(t)))
    g = -jnp.exp(gp_ref[0:1, :]) * softplus
    gb_ref[...] = jnp.where(lane < GDN_HEADS, jax.nn.sigmoid(ba), g)


def _even_in(x, seq, nw, sc, sh, w_cat, conv_w, gate_params):
    t_rows = x.shape[0]
    tm = _row_tile(seq, 512)
    kern = functools.partial(_even_in_kernel, tm=tm, seq=seq)
    row = lambda w: pl.BlockSpec((tm, w), lambda i: (i, 0))
    full = lambda a: pl.BlockSpec(a.shape, lambda i: (0,) * a.ndim)
    outs = [jax.ShapeDtypeStruct((t_rows, GDN_WIDTH), F32)] * 5 + [jax.ShapeDtypeStruct((t_rows, LANES), F32)]
    return pl.pallas_call(
        kern,
        grid=(t_rows // tm,),
        in_specs=[row(D_MODEL),
                  pl.BlockSpec((8, D_MODEL), lambda i: (jnp.maximum(i * (tm // 8) - 1, 0), 0)),
                  full(nw), _mod_spec(tm, seq), _mod_spec(tm, seq), full(w_cat), full(conv_w),
                  full(gate_params)],
        out_specs=[row(GDN_WIDTH)] * 5 + [row(LANES)],
        out_shape=outs,
        scratch_shapes=[pltpu.VMEM((tm + 8, QKV_W), F32)],
        compiler_params=_cparams("parallel"),
        name="even_in",
    )(x, x, nw, sc, sh, w_cat, conv_w, gate_params)


def _gdn_kernel(q_ref, k_ref, v_ref, z_ref, gb_ref, nw_ref, ya_ref, state_scr):
    c = pl.program_id(1)

    @pl.when(c == 0)
    def _():
        state_scr[...] = jnp.zeros_like(state_scr)

    gb = gb_ref[...]
    r64 = lax.broadcasted_iota(jnp.int32, (CHUNK, CHUNK), 0)
    c64 = lax.broadcasted_iota(jnp.int32, (CHUNK, CHUNK), 1)
    causal = r64 >= c64
    strict = r64 > c64
    tri = causal.astype(BF16)
    gc = sum(jnp.dot(tri, p, preferred_element_type=F32) for p in _split3(gb))
    sel = (lax.broadcasted_iota(jnp.int32, (8, LANES), 1)
           == lax.broadcasted_iota(jnp.int32, (8, LANES), 0) + GDN_HEADS).astype(BF16)
    gct = sum(lax.dot_general(sel, p, (((1,), (1,)), ((), ())), preferred_element_type=F32)
              for p in _split3(gc))
    nw = nw_ref[...]
    for hd in range(GDN_HEADS):
        lo = hd * GDN_HEAD_DIM
        sl = slice(lo, lo + GDN_HEAD_DIM)
        qh = q_ref[:, sl]
        kh = k_ref[:, sl]
        vh = v_ref[:, sl]
        beta = gb[:, hd:hd + 1]
        gcol = gc[:, GDN_HEADS + hd:GDN_HEADS + hd + 1]
        grow = gct[hd:hd + 1, :]
        decay = jnp.exp(jnp.where(causal, gcol - grow, -1e30))
        kb = kh * beta
        s_mat = jnp.where(strict, _mm_nt(kb, kh) * decay, 0.0)
        qk = jnp.where(causal, _mm_nt(qh, kh) * decay, 0.0)
        eg = jnp.exp(gcol)
        sol = jnp.concatenate([vh * beta, kb * eg], axis=1)
        p_mat = -s_mat
        n_fac = int(math.log2(CHUNK))
        for it in range(n_fac):
            sol = sol + _mm_hi(p_mat, sol)
            if it + 1 < n_fac:
                p_mat = _mm_hi(p_mat, p_mat)
        u_c = sol[:, :GDN_HEAD_DIM]
        w_c = sol[:, GDN_HEAD_DIM:]
        st = state_scr[hd]
        v_new = u_c - _mm(w_c, st)
        o = _mm(qh * eg, st) + _mm(qk, v_new)
        glast = gcol[CHUNK - 1:CHUNK, :]
        kd = kh * jnp.exp(glast - gcol)
        state_scr[hd] = st * jnp.exp(glast) + _mm_tn(kd, v_new)
        on = o * lax.rsqrt(jnp.mean(o * o, axis=-1, keepdims=True) + EPS) * nw
        ya_ref[:, sl] = (on * _silu(z_ref[:, sl])).astype(ya_ref.dtype)


def _gdn(q, k, v, z, gb, norm_w, batch, seq):
    nc = seq // CHUNK
    t_rows = q.shape[0]
    blk = lambda w: pl.BlockSpec((CHUNK, w), lambda b, c: (b * nc + c, 0))
    return pl.pallas_call(
        _gdn_kernel,
        grid=(batch, nc),
        in_specs=[blk(GDN_WIDTH)] * 4 + [blk(LANES), pl.BlockSpec((1, GDN_HEAD_DIM), lambda b, c: (0, 0))],
        out_specs=blk(GDN_WIDTH),
        out_shape=jax.ShapeDtypeStruct((t_rows, GDN_WIDTH), BF16),
        scratch_shapes=[pltpu.VMEM((GDN_HEADS, GDN_HEAD_DIM, GDN_HEAD_DIM), F32)],
        compiler_params=_cparams("parallel", "arbitrary"),
        name="gdn",
    )(q, k, v, z, gb, norm_w)


def _s5_tables(lam_re, lam_im, log_step, b_re, b_im, c_re, c_im, n_lev):
    hp = lax.Precision.HIGHEST
    lr = jnp.minimum(lam_re.astype(F32), -1e-4)
    li = lam_im.astype(F32)
    dt = jnp.exp(log_step.astype(F32))[:, None]
    js = jnp.arange(S5_Q + 1, dtype=F32)[:, None, None]
    mag = jnp.exp(lr * dt * js)
    pr, pi = mag * jnp.cos(li * dt * js), mag * jnp.sin(li * dt * js)
    ar, ai = pr[1], pi[1]
    nr, ni = ar - 1.0, ai
    den = lr * lr + li * li
    cr, ci = (nr * lr + ni * li) / den, (ni * lr - nr * li) / den
    b_re, b_im = b_re.astype(F32), b_im.astype(F32)
    bbr = cr[..., None] * b_re - ci[..., None] * b_im
    bbi = cr[..., None] * b_im + ci[..., None] * b_re
    c_re, c_im = c_re.astype(F32), c_im.astype(F32)
    lbr = pr[..., None] * bbr - pi[..., None] * bbi
    lbi = pr[..., None] * bbi + pi[..., None] * bbr
    kern = (jnp.einsum('gon,jgni->gjoi', c_re, lbr[:S5_Q], precision=hp)
            - jnp.einsum('gon,jgni->gjoi', c_im, lbi[:S5_Q], precision=hp))
    s_idx = jnp.arange(S5_Q)[:, None]
    t_idx = jnp.arange(S5_Q)[None, :]
    lag = jnp.clip(t_idx - s_idx, 0, S5_Q - 1)
    toep = jnp.where((t_idx >= s_idx)[None, :, :, None, None], kern[:, lag], 0.0)
    toep = toep.transpose(0, 1, 4, 2, 3).reshape(S5_GROUPS, S5_Q * S5_GROUP, S5_Q * S5_GROUP)
    er = lbr[:S5_Q][::-1].transpose(1, 0, 3, 2).reshape(S5_GROUPS, S5_Q * S5_GROUP, S5_STATE)
    ei = lbi[:S5_Q][::-1].transpose(1, 0, 3, 2).reshape(S5_GROUPS, S5_Q * S5_GROUP, S5_STATE)
    e_mat = jnp.concatenate([er, ei], axis=-1)
    clr = c_re[None] * pr[1:, :, None, :] - c_im[None] * pi[1:, :, None, :]
    cli = c_re[None] * pi[1:, :, None, :] + c_im[None] * pr[1:, :, None, :]
    fr = clr.transpose(1, 3, 0, 2).reshape(S5_GROUPS, S5_STATE, S5_Q * S5_GROUP)
    fi = (-cli).transpose(1, 3, 0, 2).reshape(S5_GROUPS, S5_STATE, S5_Q * S5_GROUP)
    f_mat = jnp.concatenate([fr, fi], axis=1)
    a1, a2 = [], []
    cur_r, cur_i = pr[S5_Q], pi[S5_Q]
    for _ in range(n_lev):
        a1.append(jnp.concatenate([cur_r, cur_r], axis=-1))
        a2.append(jnp.concatenate([-cur_i, cur_i], axis=-1))
        cur_r, cur_i = cur_r * cur_r - cur_i * cur_i, 2.0 * cur_r * cur_i
    pad = [jnp.zeros_like(a1[0])] * (8 - n_lev % 8 if n_lev % 8 else 0)
    a1 = jnp.stack(a1 + pad, axis=1)
    a2 = jnp.stack(a2 + pad, axis=1)
    return toep.astype(BF16), e_mat.astype(BF16), f_mat.astype(BF16), a1, a2


def _s5_conv_kernel(u_ref, t_ref, e_ref, f_ref, a1_ref, a2_ref, y_ref, *, seg, n_lev):
    u = u_ref[0]
    y1 = jnp.dot(u, t_ref[0], preferred_element_type=F32)
    x = jnp.dot(u, e_ref[0], preferred_element_type=F32)
    rin = lax.broadcasted_iota(jnp.int32, x.shape, 0) % seg
    for lev in range(n_lev):
        s = 1 << lev
        xs = jnp.where(rin >= s, pltpu.roll(x, s, 0), 0.0)
        x = x + a1_ref[0, lev:lev + 1, :] * xs + a2_ref[0, lev:lev + 1, :] * pltpu.roll(xs, S5_STATE, 1)
    xp = jnp.where(rin >= 1, pltpu.roll(x, 1, 0), 0.0)
    y_ref[0] = y1 + _mm(xp, f_ref[0])


def _s5_conv(u_grp, toep, e_mat, f_mat, a1, a2, seg, n_lev):
    g, m, w = u_grp.shape
    kern = functools.partial(_s5_conv_kernel, seg=seg, n_lev=n_lev)
    per_g = lambda a: pl.BlockSpec((1,) + a.shape[1:], lambda i: (i, 0, 0))
    return pl.pallas_call(
        kern,
        grid=(g,),
        in_specs=[per_g(u_grp), per_g(toep), per_g(e_mat), per_g(f_mat), per_g(a1), per_g(a2)],
        out_specs=pl.BlockSpec((1, m, w), lambda i: (i, 0, 0)),
        out_shape=jax.ShapeDtypeStruct((g, m, w), F32),
        compiler_params=_cparams("parallel"),
        name="s5_conv",
    )(u_grp, toep, e_mat, f_mat, a1, a2)


def _s5_glu_kernel(y_ref, u_ref, d_ref, w_ref, b_ref, o_ref):
    y = y_ref[...] + d_ref[...] * u_ref[...]
    y = 0.5 * y * (1.0 + lax.erf(y * (2.0 ** -0.5)))
    ab = _mm(y, w_ref[...]) + b_ref[...]
    o_ref[...] = (ab[:, :S5_WIDTH] * jax.nn.sigmoid(ab[:, S5_WIDTH:])).astype(o_ref.dtype)


def _s5_glu(yconv, u, d_skip, glu_w, glu_b):
    t_rows = u.shape[0]
    tm = _row_tile(t_rows, 512)
    row = pl.BlockSpec((tm, S5_WIDTH), lambda i: (i, 0))
    full = lambda a: pl.BlockSpec(a.shape, lambda i: (0,) * a.ndim)
    return pl.pallas_call(
        _s5_glu_kernel,
        grid=(t_rows // tm,),
        in_specs=[row, row, full(d_skip), full(glu_w), full(glu_b)],
        out_specs=row,
        out_shape=jax.ShapeDtypeStruct((t_rows, S5_WIDTH), BF16),
        compiler_params=_cparams("parallel"),
        name="s5_glu",
    )(yconv, u, d_skip, glu_w, glu_b)


def _proj_res_kernel(*refs, n_in):
    x_ref, g_ref = refs[0], refs[1]
    ys = refs[2:2 + n_in]
    ws = refs[2 + n_in:2 + 2 * n_in]
    o_ref = refs[2 + 2 * n_in]
    acc = jnp.dot(ys[0][...], ws[0][...], preferred_element_type=F32)
    for y, w in zip(ys[1:], ws[1:]):
        acc = acc + jnp.dot(y[...], w[...], preferred_element_type=F32)
    o_ref[...] = x_ref[...] + g_ref[0] * acc


def _proj_res(x, gate, seq, ys, ws):
    t_rows = x.shape[0]
    tm = _row_tile(seq, 512)
    n_in = len(ys)
    full = lambda a: pl.BlockSpec(a.shape, lambda i: (0,) * a.ndim)
    return pl.pallas_call(
        functools.partial(_proj_res_kernel, n_in=n_in),
        grid=(t_rows // tm,),
        in_specs=([pl.BlockSpec((tm, D_MODEL), lambda i: (i, 0)), _mod_spec(tm, seq)]
                  + [pl.BlockSpec((tm, y.shape[1]), lambda i: (i, 0)) for y in ys]
                  + [full(w) for w in ws]),
        out_specs=pl.BlockSpec((tm, D_MODEL), lambda i: (i, 0)),
        out_shape=jax.ShapeDtypeStruct(x.shape, F32),
        compiler_params=_cparams("parallel"),
        name="proj_res",
    )(x, gate, *ys, *ws)


def _ffn_kernel(x_ref, nw_ref, sc_ref, sh_ref, g_ref, w1_ref, w3_ref, w2_ref, o_ref):
    x = x_ref[...]
    h = _norm_mod(x, nw_ref[...], sc_ref[0], sh_ref[0]).astype(BF16)
    a = jnp.dot(h, w1_ref[...], preferred_element_type=F32)
    b = jnp.dot(h, w3_ref[...], preferred_element_type=F32)
    f = jnp.dot((_silu(a) * b).astype(BF16), w2_ref[...], preferred_element_type=F32)
    o_ref[...] = x + g_ref[0] * f


def _ffn(x, seq, nw, sc, sh, gate, w1, w3, w2):
    t_rows = x.shape[0]
    tm = _row_tile(seq, 256)
    row = pl.BlockSpec((tm, D_MODEL), lambda i: (i, 0))
    once = lambda a: pl.BlockSpec(a.shape, lambda i: (0,) * a.ndim, pipeline_mode=pl.Buffered(1))
    return pl.pallas_call(
        _ffn_kernel,
        grid=(t_rows // tm,),
        in_specs=[row, once(nw), _mod_spec(tm, seq), _mod_spec(tm, seq), _mod_spec(tm, seq),
                  once(w1), once(w3), once(w2)],
        out_specs=row,
        out_shape=jax.ShapeDtypeStruct(x.shape, F32),
        compiler_params=_cparams("parallel"),
        name="ffn",
    )(x, nw, sc, sh, gate, w1, w3, w2)


def _odd_qkv_kernel(x_ref, nw_ref, sc_ref, sh_ref, w_ref, qw_ref, kw_ref, q_ref, k_ref, v_ref):
    h = _norm_mod(x_ref[...], nw_ref[...], sc_ref[0], sh_ref[0]).astype(BF16)
    proj = jnp.dot(h, w_ref[...], preferred_element_type=F32)
    r = lax.broadcasted_iota(jnp.int32, (LANES, LANES), 0) // DIFF_HEAD_DIM
    c = lax.broadcasted_iota(jnp.int32, (LANES, LANES), 1) // DIFF_HEAD_DIM
    grp = (r == c).astype(BF16)
    for base, w_vec, scale, out in ((0, qw_ref, DIFF_HEAD_DIM ** -0.5, q_ref), (D_MODEL, kw_ref, 1.0, k_ref)):
        for t in range(D_MODEL // LANES):
            xt = proj[:, base + t * LANES:base + (t + 1) * LANES]
            ss = sum(jnp.dot(p, grp, preferred_element_type=F32) for p in _split3(xt * xt))
            y = xt * lax.rsqrt(ss * (1.0 / DIFF_HEAD_DIM) + EPS) * w_vec[...]
            out[:, t * LANES:(t + 1) * LANES] = (y * scale).astype(out.dtype)
    v_ref[...] = proj[:, 2 * D_MODEL:].astype(v_ref.dtype)


def _odd_qkv(x, seq, nw, sc, sh, w_qkv, qw, kw):
    t_rows = x.shape[0]
    tm = _row_tile(seq, 512)
    row = pl.BlockSpec((tm, D_MODEL), lambda i: (i, 0))
    full = lambda a: pl.BlockSpec(a.shape, lambda i: (0,) * a.ndim)
    out = jax.ShapeDtypeStruct((t_rows, D_MODEL), BF16)
    return pl.pallas_call(
        _odd_qkv_kernel,
        grid=(t_rows // tm,),
        in_specs=[row, full(nw), _mod_spec(tm, seq), _mod_spec(tm, seq), full(w_qkv), full(qw), full(kw)],
        out_specs=[row, row, row],
        out_shape=[out, out, out],
        compiler_params=_cparams("parallel"),
        name="odd_qkv",
    )(x, nw, sc, sh, w_qkv, qw, kw)


def _attn_kernel(q_ref, k_ref, v_ref, lam_ref, sw_ref, o_ref, acc0, acc1, *, tq, out_scale):
    i = pl.program_id(2)
    q = q_ref[...]
    lane = lax.broadcasted_iota(jnp.int32, q.shape, 1)
    zero = jnp.zeros_like(q)
    qs = (jnp.where(lane < DIFF_HEAD_DIM, q, zero), jnp.where(lane >= DIFF_HEAD_DIM, q, zero))
    accs = (acc0, acc1)
    acc0[...] = jnp.zeros_like(acc0)
    acc1[...] = jnp.zeros_like(acc1)
    neg = jnp.full((tq, 1), -1e30, F32)
    zer = jnp.zeros((tq, 1), F32)

    def block(j, carry, masked):
        kb = k_ref[pl.ds(pl.multiple_of(j * tq, tq), tq), :]
        vb = v_ref[pl.ds(pl.multiple_of(j * tq, tq), tq), :]
        out = []
        for t in range(2):
            m, l = carry[2 * t], carry[2 * t + 1]
            s = lax.dot_general(qs[t], kb, (((1,), (1,)), ((), ())), preferred_element_type=F32)
            if masked:
                rr = lax.broadcasted_iota(jnp.int32, s.shape, 0) // CHUNK
                cc = lax.broadcasted_iota(jnp.int32, s.shape, 1) // CHUNK
                s = jnp.where(cc <= rr, s, -1e30)
            m_new = jnp.maximum(m, jnp.max(s, axis=-1, keepdims=True))
            p = jnp.exp(s - m_new)
            alpha = jnp.exp(m - m_new)
            l_new = alpha * l + jnp.sum(p, axis=-1, keepdims=True)
            accs[t][...] = alpha * accs[t][...] + jnp.dot(p.astype(BF16), vb, preferred_element_type=F32)
            out += [m_new, l_new]
        return tuple(out)

    carry = lax.fori_loop(0, i, lambda j, c: block(j, c, False), (neg, zer, neg, zer))
    m0, l0, m1, l1 = block(i, carry, True)
    lam = lam_ref[0:1, 0:1]
    o = acc0[...] / l0 - lam * (acc1[...] / l1)
    o = o * lax.rsqrt(jnp.mean(o * o, axis=-1, keepdims=True) + EPS) * sw_ref[...] * out_scale
    o_ref[...] = o.astype(o_ref.dtype)


def _attention(q, k, v, lam_vec, subln_w, batch, seq, out_scale):
    tq = _row_tile(seq, 256)
    nq = seq // tq
    kern = functools.partial(_attn_kernel, tq=tq, out_scale=out_scale)
    qspec = pl.BlockSpec((tq, LANES), lambda b, h, i: (b * nq + i, h))
    kvspec = pl.BlockSpec((seq, LANES), lambda b, h, i: (b, h))
    full = lambda a: pl.BlockSpec(a.shape, lambda b, h, i: (0,) * a.ndim)
    return pl.pallas_call(
        kern,
        grid=(batch, DIFF_HEADS, nq),
        in_specs=[qspec, kvspec, kvspec, full(lam_vec), full(subln_w)],
        out_specs=qspec,
        out_shape=jax.ShapeDtypeStruct(q.shape, BF16),
        scratch_shapes=[pltpu.VMEM((tq, LANES), F32), pltpu.VMEM((tq, LANES), F32)],
        compiler_params=_cparams("parallel", "parallel", "arbitrary"),
        name="diff_attn",
    )(q, k, v, lam_vec, subln_w)


def _router_kernel(x_ref, nw_ref, sc_ref, sh_ref, whi_ref, wlo_ref, h_ref, idx_ref, gate_ref):
    h = _norm_mod(x_ref[...], nw_ref[...], sc_ref[0], sh_ref[0])
    h_ref[...] = h
    h_hi = h.astype(BF16)
    h_lo = (h - h_hi.astype(F32)).astype(BF16)
    logits = (jnp.dot(h_hi, whi_ref[...], preferred_element_type=F32)
              + jnp.dot(h_hi, wlo_ref[...], preferred_element_type=F32)
              + jnp.dot(h_lo, whi_ref[...], preferred_element_type=F32))
    lane = lax.broadcasted_iota(jnp.int32, logits.shape, 1)
    lane_f = lane.astype(F32)
    logits = jnp.where(lane < N_EXPERTS, logits, -jnp.inf)
    m1 = jnp.max(logits, axis=-1, keepdims=True)
    i1 = jnp.min(jnp.where(logits == m1, lane_f, float(LANES)), axis=-1, keepdims=True)
    rest = jnp.where(lane_f == i1, -jnp.inf, logits)
    m2 = jnp.max(rest, axis=-1, keepdims=True)
    i2 = jnp.min(jnp.where(rest == m2, lane_f, float(LANES)), axis=-1, keepdims=True)
    e = jnp.exp(m2 - m1)
    g1 = 1.0 / (1.0 + e)
    idx_ref[...] = jnp.where(lane == 0, i1, jnp.where(lane == 1, i2, 0.0)).astype(jnp.int32)
    gate_ref[...] = jnp.where(lane == 0, g1, jnp.where(lane == 1, e * g1, 0.0))


def _router(x, seq, nw, sc, sh, w_hi, w_lo):
    t_rows = x.shape[0]
    tm = _row_tile(seq, 512)
    row = lambda w: pl.BlockSpec((tm, w), lambda i: (i, 0))
    full = lambda a: pl.BlockSpec(a.shape, lambda i: (0,) * a.ndim)
    return pl.pallas_call(
        _router_kernel,
        grid=(t_rows // tm,),
        in_specs=[row(D_MODEL), full(nw), _mod_spec(tm, seq), _mod_spec(tm, seq), full(w_hi), full(w_lo)],
        out_specs=[row(D_MODEL), row(LANES), row(LANES)],
        out_shape=[jax.ShapeDtypeStruct((t_rows, D_MODEL), F32),
                   jax.ShapeDtypeStruct((t_rows, LANES), jnp.int32),
                   jax.ShapeDtypeStruct((t_rows, LANES), F32)],
        compiler_params=_cparams("parallel"),
        name="router",
    )(x, nw, sc, sh, w_hi, w_lo)


GATHER_ROWS = 256


def _row_copy(src_hbm, dst_ref, sem, src_row, dst_row):
    return pltpu.make_async_copy(src_hbm.at[pl.ds(src_row, 1), :], dst_ref.at[pl.ds(dst_row, 1), :], sem)


def _gather_kernel(idx_ref, src_hbm, o_ref, sem):
    def start(r, _):
        _row_copy(src_hbm, o_ref, sem, idx_ref[0, 0, r], r).start()
        return 0

    lax.fori_loop(0, GATHER_ROWS, start, 0)

    def wait(r, _):
        _row_copy(src_hbm, o_ref, sem, 0, r).wait()
        return 0

    lax.fori_loop(0, GATHER_ROWS, wait, 0)


def _gather_rows(src, idx):
    n = idx.shape[0]
    width = src.shape[1]
    return pl.pallas_call(
        _gather_kernel,
        grid=(n // GATHER_ROWS,),
        in_specs=[pl.BlockSpec((1, 1, GATHER_ROWS), lambda i: (i, 0, 0), memory_space=pltpu.SMEM),
                  pl.BlockSpec(memory_space=pl.ANY)],
        out_specs=pl.BlockSpec((GATHER_ROWS, width), lambda i: (i, 0)),
        scratch_shapes=[pltpu.SemaphoreType.DMA(())],
        out_shape=jax.ShapeDtypeStruct((n, width), src.dtype),
        compiler_params=_cparams("arbitrary"),
        name="gather_rows",
    )(idx.reshape(n // GATHER_ROWS, 1, GATHER_ROWS), src)


def _expert_kernel(te_ref, nt_ref, x_ref, w1_ref, w3_ref, w2_ref, o_ref, acc_scr):
    i = pl.program_id(0)
    f = pl.program_id(1)

    @pl.when(i < nt_ref[0])
    def _():
        h = x_ref[...].astype(BF16)
        a = jnp.dot(h, w1_ref[0], preferred_element_type=F32)
        b = jnp.dot(h, w3_ref[0], preferred_element_type=F32)
        part = jnp.dot((_silu(a) * b).astype(BF16), w2_ref[0], preferred_element_type=F32)

        @pl.when(f == 0)
        def _():
            acc_scr[...] = part

        @pl.when(f > 0)
        def _():
            acc_scr[...] = acc_scr[...] + part

    @pl.when(f == pl.num_programs(1) - 1)
    def _():
        o_ref[...] = jnp.where(i < nt_ref[0], acc_scr[...], 0.0)


EXPERT_TM = 512
EXPERT_TF = D_FF // 2


def _expert_ffn(x_perm, tile_expert, n_tiles_used, w13, w2):
    p_rows = x_perm.shape[0]
    tm, tf = EXPERT_TM, EXPERT_TF
    nf = D_FF // tf
    return pl.pallas_call(
        _expert_kernel,
        grid_spec=pltpu.PrefetchScalarGridSpec(
            num_scalar_prefetch=2,
            grid=(p_rows // tm, nf),
            in_specs=[pl.BlockSpec((tm, D_MODEL), lambda i, f, te, nt: (i, 0)),
                      pl.BlockSpec((1, D_MODEL, tf), lambda i, f, te, nt: (te[i], 0, f)),
                      pl.BlockSpec((1, D_MODEL, tf), lambda i, f, te, nt: (te[i], 0, f + nf)),
                      pl.BlockSpec((1, tf, D_MODEL), lambda i, f, te, nt: (te[i], f, 0))],
            out_specs=pl.BlockSpec((tm, D_MODEL), lambda i, f, te, nt: (i, 0)),
            scratch_shapes=[pltpu.VMEM((tm, D_MODEL), F32)],
        ),
        out_shape=jax.ShapeDtypeStruct((p_rows, D_MODEL), F32),
        compiler_params=_cparams("parallel", "arbitrary"),
        name="expert_ffn",
    )(tile_expert, n_tiles_used, x_perm, w13, w13, w2)


def _combine_kernel(x_ref, g_ref, gate_ref, y0_ref, y1_ref, o_ref):
    gates = gate_ref[...]
    f = gates[:, 0:1] * y0_ref[...] + gates[:, 1:2] * y1_ref[...]
    o_ref[...] = x_ref[...] + g_ref[0] * f


def _combine(x, gate_mod, seq, gates, y_tok):
    t_rows = x.shape[0]
    tm = _row_tile(seq, 512)
    nt = t_rows // tm
    row = pl.BlockSpec((tm, D_MODEL), lambda i: (i, 0))
    return pl.pallas_call(
        _combine_kernel,
        grid=(nt,),
        in_specs=[row, _mod_spec(tm, seq), pl.BlockSpec((tm, LANES), lambda i: (i, 0)),
                  row, pl.BlockSpec((tm, D_MODEL), lambda i: (i + nt, 0))],
        out_specs=row,
        out_shape=jax.ShapeDtypeStruct(x.shape, F32),
        compiler_params=_cparams("parallel"),
        name="moe_combine",
    )(x, gate_mod, gates, y_tok, y_tok)


def _dispatch_plan(expert_idx, t_rows):
    tm = EXPERT_TM
    flat = expert_idx.reshape(-1)
    onehot = (flat[:, None] == jnp.arange(N_EXPERTS)[None, :]).astype(jnp.int32)
    rank = jnp.take_along_axis(jnp.cumsum(onehot, axis=0) - onehot, flat[:, None], axis=1)[:, 0]
    counts = jnp.sum(onehot, axis=0)
    tiles_per = (counts + tm - 1) // tm
    tile_end = jnp.cumsum(tiles_per)
    offs = (tile_end - tiles_per) * tm
    pos = offs[flat] + rank
    n_tiles = (2 * t_rows) // tm + N_EXPERTS
    p_rows = n_tiles * tm
    tok_of_slot = jnp.zeros((p_rows,), jnp.int32).at[pos].set(jnp.arange(2 * t_rows, dtype=jnp.int32) // 2)
    tile_expert = jnp.minimum(
        jnp.searchsorted(tile_end, jnp.arange(n_tiles, dtype=jnp.int32), side='right'), N_EXPERTS - 1
    ).astype(jnp.int32)
    n_used = tile_end[-1:].astype(jnp.int32)
    pos2 = pos.reshape(t_rows, 2)
    gather_back = jnp.concatenate([pos2[:, 0], pos2[:, 1]]).astype(jnp.int32)
    return tok_of_slot, tile_expert, n_used, gather_back


def _mods(mod, layer, batch):
    m = mod[layer, :batch].reshape(batch, 6, 1, D_MODEL)
    return [m[:, j] for j in range(6)]


def _even_layer(x, batch, seq, mods, nw_mix, nw_ffn, w_in, conv_w, a_log, dt_bias, gdn_norm_w,
                lam_re, lam_im, log_step, b_re, b_im, c_re, c_im, d_skip, glu_w, glu_b, w_out, w13, w2):
    sh1, sc1, g1, sh2, sc2, g2 = mods
    t_rows = batch * seq
    qkvz, rest = w_in[:, :4 * GDN_WIDTH], w_in[:, 4 * GDN_WIDTH:]
    w_ba, w_u = rest[:, :2 * GDN_HEADS], rest[:, 2 * GDN_HEADS:]
    w_cat = jnp.concatenate(
        [qkvz, w_u, w_ba, jnp.zeros((D_MODEL, LANES - 2 * GDN_HEADS), w_in.dtype)], axis=1).astype(BF16)
    gate_params = jnp.zeros((8, LANES), F32)
    gate_params = gate_params.at[0, GDN_HEADS:2 * GDN_HEADS].set(a_log.astype(F32))
    gate_params = gate_params.at[1, GDN_HEADS:2 * GDN_HEADS].set(dt_bias.astype(F32))
    q, k, v, z, u, gb = _even_in(x, seq, nw_mix, sc1, sh1, w_cat, conv_w.astype(F32), gate_params)
    y_a = _gdn(q, k, v, z, gb, gdn_norm_w.reshape(1, GDN_HEAD_DIM).astype(F32), batch, seq)

    seg = seq // S5_Q
    n_lev = max(int(math.ceil(math.log2(seg))), 0)
    toep, e_mat, f_mat, a1, a2 = _s5_tables(lam_re, lam_im, log_step, b_re, b_im, c_re, c_im, max(n_lev, 1))
    m_rows = t_rows // S5_Q
    u_grp = (u.reshape(m_rows, S5_Q, S5_GROUPS, S5_GROUP).transpose(2, 0, 1, 3)
             .reshape(S5_GROUPS, m_rows, S5_Q * S5_GROUP).astype(BF16))
    yg = _s5_conv(u_grp, toep, e_mat, f_mat, a1, a2, seg, n_lev)
    yconv = (yg.reshape(S5_GROUPS, m_rows, S5_Q, S5_GROUP).transpose(1, 2, 0, 3)
             .reshape(t_rows, S5_WIDTH))
    y_b = _s5_glu(yconv, u, d_skip.reshape(1, S5_WIDTH).astype(F32), glu_w.astype(BF16),
                  glu_b.reshape(1, 2 * S5_WIDTH).astype(F32))
    wo = w_out.astype(BF16)
    x = _proj_res(x, g1, seq, [y_a, y_b], [wo[:GDN_WIDTH], wo[GDN_WIDTH:]])
    w13b = w13.astype(BF16)
    return _ffn(x, seq, nw_ffn, sc2, sh2, g2, w13b[:, :D_FF], w13b[:, D_FF:], w2.astype(BF16))


def _odd_layer(x, batch, seq, mods, nw_mix, nw_ffn, w_qkv, q_norm_w, k_norm_w, lq1, lk1, lq2, lk2,
               subln_w, w_out, router_w, w13, w2, lambda_init):
    sh1, sc1, g1, sh2, sc2, g2 = mods
    t_rows = batch * seq
    tile2 = lambda w: jnp.tile(w.astype(F32), LANES // DIFF_HEAD_DIM).reshape(1, LANES)
    q, k, v = _odd_qkv(x, seq, nw_mix, sc1, sh1, w_qkv.astype(BF16), tile2(q_norm_w), tile2(k_norm_w))
    lam = (jnp.exp(jnp.sum(lq1.astype(F32) * lk1.astype(F32)))
           - jnp.exp(jnp.sum(lq2.astype(F32) * lk2.astype(F32))) + lambda_init)
    lam_vec = jnp.full((8, LANES), lam, F32)
    o = _attention(q, k, v, lam_vec, subln_w.reshape(1, LANES).astype(F32), batch, seq, 1.0 - lambda_init)
    x = _proj_res(x, g1, seq, [o], [w_out.astype(BF16)])

    rw = jnp.zeros((D_MODEL, LANES), F32).at[:, :N_EXPERTS].set(router_w.astype(F32))
    rw_hi = rw.astype(BF16)
    rw_lo = (rw - rw_hi.astype(F32)).astype(BF16)
    h, idx, gates = _router(x, seq, nw_ffn, sc2, sh2, rw_hi, rw_lo)
    tok_of_slot, tile_expert, n_used, gather_back = _dispatch_plan(idx[:, :2], t_rows)
    h_perm = _gather_rows(h, tok_of_slot)
    y_perm = _expert_ffn(h_perm, tile_expert, n_used, w13.astype(BF16), w2.astype(BF16))
    y_tok = _gather_rows(y_perm, gather_back)
    return _combine(x, g2, seq, gates, y_tok)


def kernel(x, c, ada_w, ada_b, norm_mix_w, norm_ffn_w, even_w_in, even_conv_w, even_a_log, even_dt_bias, even_gdn_norm_w, even_lam_re, even_lam_im, even_log_step, even_b_re, even_b_im, even_c_re, even_c_im, even_d_skip, even_glu_w, even_glu_b, even_w_out, even_ffn_w13, even_ffn_w2, odd_w_qkv, odd_q_norm_w, odd_k_norm_w, odd_lambda_q1, odd_lambda_k1, odd_lambda_q2, odd_lambda_k2, odd_subln_w, odd_w_out, odd_router_w, odd_expert_w13, odd_expert_w2):
    batch, seq, d = x.shape
    assert d == D_MODEL and seq % CHUNK == 0 and seq % S5_Q == 0
    c_pad = jnp.zeros((8, d), F32).at[:batch].set(c.astype(F32))
    mod = _adaln(c_pad, ada_w.astype(BF16), ada_b.astype(F32))
    xf = x.astype(F32).reshape(batch * seq, d)
    for layer in range(DEPTH):
        i = layer // 2
        mods = _mods(mod, layer, batch)
        nw_mix = norm_mix_w[layer].reshape(1, d).astype(F32)
        nw_ffn = norm_ffn_w[layer].reshape(1, d).astype(F32)
        if layer % 2 == 0:
            xf = _even_layer(xf, batch, seq, mods, nw_mix, nw_ffn, even_w_in[i], even_conv_w[i], even_a_log[i],
                             even_dt_bias[i], even_gdn_norm_w[i], even_lam_re[i], even_lam_im[i],
                             even_log_step[i], even_b_re[i], even_b_im[i], even_c_re[i], even_c_im[i],
                             even_d_skip[i], even_glu_w[i], even_glu_b[i], even_w_out[i],
                             even_ffn_w13[i], even_ffn_w2[i])
        else:
            lambda_init = 0.8 - 0.6 * math.exp(-0.3 * layer)
            xf = _odd_layer(xf, batch, seq, mods, nw_mix, nw_ffn, odd_w_qkv[i], odd_q_norm_w[i], odd_k_norm_w[i],
                            odd_lambda_q1[i], odd_lambda_k1[i], odd_lambda_q2[i], odd_lambda_k2[i],
                            odd_subln_w[i], odd_w_out[i], odd_router_w[i], odd_expert_w13[i],
                            odd_expert_w2[i], lambda_init)
    return xf.reshape(batch, seq, d).astype(x.dtype)
```

```python
import functools
import math

import jax
import jax.numpy as jnp
from jax import lax
from jax.experimental import pallas as pl
from jax.experimental.pallas import tpu as pltpu

F32 = jnp.float32
BF16 = jnp.bfloat16

D_MODEL = 1024
DEPTH = 4
CHUNK = 64
EPS = 1e-6
GDN_HEAD_DIM = 128
GDN_WIDTH = D_MODEL // 2
GDN_HEADS = GDN_WIDTH // GDN_HEAD_DIM
CONV_K = 4
QKV_W = 3 * GDN_WIDTH
S5_WIDTH = D_MODEL - GDN_WIDTH
S5_GROUP = 16
S5_GROUPS = S5_WIDTH // S5_GROUP
S5_STATE = 64
S5_Q = 16
DIFF_HEAD_DIM = 64
DIFF_HEADS = D_MODEL // (2 * DIFF_HEAD_DIM)
D_FF = ((8 * D_MODEL // 3 + 127) // 128) * 128
N_EXPERTS = 8
LANES = 128
EVEN_IN_COLS = QKV_W + GDN_WIDTH + S5_WIDTH + LANES
VMEM_LIMIT = 56 * 1024 * 1024


def _cparams(*sem):
    return pltpu.CompilerParams(dimension_semantics=sem, vmem_limit_bytes=VMEM_LIMIT)


def _mm(a, b):
    return jnp.dot(a.astype(BF16), b.astype(BF16), preferred_element_type=F32)


def _mm_nt(a, b):
    return lax.dot_general(a.astype(BF16), b.astype(BF16), (((1,), (1,)), ((), ())),
                           preferred_element_type=F32)


def _mm_tn(a, b):
    return lax.dot_general(a.astype(BF16), b.astype(BF16), (((0,), (0,)), ((), ())),
                           preferred_element_type=F32)


def _mm_hi(a, b):
    a_hi = a.astype(BF16)
    a_lo = (a - a_hi.astype(F32)).astype(BF16)
    b_hi = b.astype(BF16)
    b_lo = (b - b_hi.astype(F32)).astype(BF16)
    return (jnp.dot(a_hi, b_hi, preferred_element_type=F32)
            + (jnp.dot(a_hi, b_lo, preferred_element_type=F32) + jnp.dot(a_lo, b_hi, preferred_element_type=F32)))


def _split3(x):
    hi = x.astype(BF16)
    r = x - hi.astype(F32)
    mid = r.astype(BF16)
    lo = (r - mid.astype(F32)).astype(BF16)
    return hi, mid, lo


def _norm_mod(x, nw, sc, sh):
    ms = jnp.mean(x * x, axis=-1, keepdims=True)
    return (x * lax.rsqrt(ms + EPS) * nw) * (1.0 + sc) + sh


def _silu(x):
    return x * jax.nn.sigmoid(x)


def _row_tile(n, pref):
    t = min(pref, n)
    while n % t:
        t //= 2
    return t


def _adaln_kernel(c_ref, w_ref, b_ref, o_ref):
    c = c_ref[...]
    o_ref[0] = _mm(_silu(c), w_ref[0]) + b_ref[0]


def _adaln(c_pad, ada_w, ada_b):
    depth, d, n = ada_w.shape
    tn = 1536
    rows = c_pad.shape[0]
    return pl.pallas_call(
        _adaln_kernel,
        grid=(depth, n // tn),
        in_specs=[pl.BlockSpec((rows, d), lambda l, j: (0, 0)),
                  pl.BlockSpec((1, d, tn), lambda l, j: (l, 0, j)),
                  pl.BlockSpec((1, 1, tn), lambda l, j: (l, 0, j))],
        out_specs=pl.BlockSpec((1, rows, tn), lambda l, j: (l, 0, j)),
        out_shape=jax.ShapeDtypeStruct((depth, rows, n), F32),
        compiler_params=_cparams("parallel", "parallel"),
        name="adaln",
    )(c_pad, ada_w, ada_b.reshape(depth, 1, n))


def _mod_spec(tm, seq):
    return pl.BlockSpec((1, 1, D_MODEL), lambda i, *_: ((i * tm) // seq, 0, 0))


def _even_in_kernel(x_ref, xh_ref, nw_ref, sc_ref, sh_ref, w_ref, cw_ref, gp_ref,
                    q_ref, k_ref, v_ref, z_ref, u_ref, gb_ref, pre_scr, *, tm, seq):
    i = pl.program_id(0)
    nw = nw_ref[...]
    sc = sc_ref[0]
    sh = sh_ref[0]
    h = _norm_mod(x_ref[...], nw, sc, sh).astype(BF16)
    proj = jnp.dot(h, w_ref[...], preferred_element_type=F32)
    hh = _norm_mod(xh_ref[...], nw, sc, sh).astype(BF16)
    preh = jnp.dot(hh, w_ref[:, :QKV_W], preferred_element_type=F32)
    preh = jnp.where((i * tm) % seq == 0, 0.0, preh)
    pre = proj[:, :QKV_W]
    pre_scr[0:8, :] = preh
    pre_scr[8:8 + tm, :] = pre
    acc = pre * cw_ref[CONV_K - 1:CONV_K, :]
    for j in range(CONV_K - 1):
        acc = acc + pre_scr[pl.ds(8 - (CONV_K - 1) + j, tm), :] * cw_ref[j:j + 1, :]
    qkv = _silu(acc)
    for hd in range(GDN_HEADS):
        lo = hd * GDN_HEAD_DIM
        qh = qkv[:, lo:lo + GDN_HEAD_DIM]
        kh = qkv[:, GDN_WIDTH + lo:GDN_WIDTH + lo + GDN_HEAD_DIM]
        q_ref[:, lo:lo + GDN_HEAD_DIM] = (
            qh * lax.rsqrt(jnp.sum(qh * qh, axis=-1, keepdims=True) + EPS) * GDN_HEAD_DIM ** -0.5)
        k_ref[:, lo:lo + GDN_HEAD_DIM] = kh * lax.rsqrt(jnp.sum(kh * kh, axis=-1, keepdims=True) + EPS)
    v_ref[...] = qkv[:, 2 * GDN_WIDTH:]
    z_ref[...] = proj[:, QKV_W:QKV_W + GDN_WIDTH]
    u_ref[...] = proj[:, QKV_W + GDN_WIDTH:QKV_W + GDN_WIDTH + S5_WIDTH]
    ba = proj[:, QKV_W + GDN_WIDTH + S5_WIDTH:]
    lane = lax.broadcasted_iota(jnp.int32, ba.shape, 1)
    t = ba + gp_ref[1:2, :]
    softplus = jnp.maximum(t, 0.0) + jnp.log1p(jnp.exp(-jnp.abs(t)))
    g = -jnp.exp(gp_ref[0:1, :]) * softplus
    gb_ref[...] = jnp.where(lane < GDN_HEADS, jax.nn.sigmoid(ba), g)


def _even_in(x, seq, nw, sc, sh, w_cat, conv_w, gate_params):
    t_rows = x.shape[0]
    tm = _row_tile(seq, 512)
    kern = functools.partial(_even_in_kernel, tm=tm, seq=seq)
    row = lambda w: pl.BlockSpec((tm, w), lambda i: (i, 0))
    full = lambda a: pl.BlockSpec(a.shape, lambda i: (0,) * a.ndim)
    outs = [jax.ShapeDtypeStruct((t_rows, GDN_WIDTH), F32)] * 5 + [jax.ShapeDtypeStruct((t_rows, LANES), F32)]
    return pl.pallas_call(
        kern,
        grid=(t_rows // tm,),
        in_specs=[row(D_MODEL),
                  pl.BlockSpec((8, D_MODEL), lambda i: (jnp.maximum(i * (tm // 8) - 1, 0), 0)),
                  full(nw), _mod_spec(tm, seq), _mod_spec(tm, seq), full(w_cat), full(conv_w),
                  full(gate_params)],
        out_specs=[row(GDN_WIDTH)] * 5 + [row(LANES)],
        out_shape=outs,
        scratch_shapes=[pltpu.VMEM((tm + 8, QKV_W), F32)],
        compiler_params=_cparams("parallel"),
        name="even_in",
    )(x, x, nw, sc, sh, w_cat, conv_w, gate_params)


def _gdn_kernel(q_ref, k_ref, v_ref, z_ref, gb_ref, nw_ref, ya_ref, state_scr, *, batch):
    @pl.when(pl.program_id(0) == 0)
    def _():
        state_scr[...] = jnp.zeros_like(state_scr)

    r64 = lax.broadcasted_iota(jnp.int32, (CHUNK, CHUNK), 0)
    c64 = lax.broadcasted_iota(jnp.int32, (CHUNK, CHUNK), 1)
    causal = r64 >= c64
    strict = r64 > c64
    tri = causal.astype(BF16)
    sel = (lax.broadcasted_iota(jnp.int32, (8, LANES), 1)
           == lax.broadcasted_iota(jnp.int32, (8, LANES), 0) + GDN_HEADS).astype(BF16)
    nw = nw_ref[...]
    units = [(b, hd) for b in range(batch) for hd in range(GDN_HEADS)]
    gbs = [gb_ref[b] for b in range(batch)]
    gcs = [sum(jnp.dot(tri, p, preferred_element_type=F32) for p in _split3(gb)) for gb in gbs]
    gcts = [sum(lax.dot_general(sel, p, (((1,), (1,)), ((), ())), preferred_element_type=F32)
                for p in _split3(gc)) for gc in gcs]

    qs, ks, gcols, egs, kbs = [], [], [], [], []
    for b, hd in units:
        sl = slice(hd * GDN_HEAD_DIM, (hd + 1) * GDN_HEAD_DIM)
        qs.append(q_ref[b, :, sl])
        ks.append(k_ref[b, :, sl])
        gcols.append(gcs[b][:, GDN_HEADS + hd:GDN_HEADS + hd + 1])
        egs.append(jnp.exp(gcols[-1]))
        kbs.append(ks[-1] * gbs[b][:, hd:hd + 1])
    kks = [_mm_nt(kb, kh) for kb, kh in zip(kbs, ks)]
    qks = [_mm_nt(qh, kh) for qh, kh in zip(qs, ks)]
    p_mats, sols = [], []
    for u, (b, hd) in enumerate(units):
        sl = slice(hd * GDN_HEAD_DIM, (hd + 1) * GDN_HEAD_DIM)
        decay = jnp.exp(jnp.where(causal, gcols[u] - gcts[b][hd:hd + 1, :], -1e30))
        p_mats.append(jnp.where(strict, -kks[u] * decay, 0.0))
        qks[u] = jnp.where(causal, qks[u] * decay, 0.0)
        sols.append(jnp.concatenate([v_ref[b, :, sl] * gbs[b][:, hd:hd + 1], kbs[u] * egs[u]], axis=1))
    n_fac = int(math.log2(CHUNK))
    for it in range(n_fac):
        sols = [sol + _mm_hi(p, sol) for p, sol in zip(p_mats, sols)]
        if it + 1 < n_fac:
            p_mats = [_mm_hi(p, p) for p in p_mats]
    sts = [state_scr[u] for u in range(len(units))]
    ws = [_mm(sol[:, GDN_HEAD_DIM:], st) for sol, st in zip(sols, sts)]
    os_ = [_mm(qh * eg, st) for qh, eg, st in zip(qs, egs, sts)]
    v_news = [sol[:, :GDN_HEAD_DIM] - w for sol, w in zip(sols, ws)]
    os_ = [o + _mm(qk, vn) for o, qk, vn in zip(os_, qks, v_news)]
    for u, (b, hd) in enumerate(units):
        sl = slice(hd * GDN_HEAD_DIM, (hd + 1) * GDN_HEAD_DIM)
        glast = gcols[u][CHUNK - 1:CHUNK, :]
        kd = ks[u] * jnp.exp(glast - gcols[u])
        state_scr[u] = sts[u] * jnp.exp(glast) + _mm_tn(kd, v_news[u])
        o = os_[u]
        on = o * lax.rsqrt(jnp.mean(o * o, axis=-1, keepdims=True) + EPS) * nw
        ya_ref[b, :, sl] = (on * _silu(z_ref[b, :, sl])).astype(ya_ref.dtype)


def _gdn(q, k, v, z, gb, norm_w, batch, seq):
    nc = seq // CHUNK
    t_rows = q.shape[0]
    blk = lambda w: pl.BlockSpec((batch, CHUNK, w), lambda c: (0, c, 0))
    r3 = lambda a: a.reshape(batch, seq, a.shape[-1])
    out = pl.pallas_call(
        functools.partial(_gdn_kernel, batch=batch),
        grid=(nc,),
        in_specs=[blk(GDN_WIDTH)] * 4 + [blk(LANES), pl.BlockSpec((1, GDN_HEAD_DIM), lambda c: (0, 0))],
        out_specs=blk(GDN_WIDTH),
        out_shape=jax.ShapeDtypeStruct((batch, seq, GDN_WIDTH), BF16),
        scratch_shapes=[pltpu.VMEM((batch * GDN_HEADS, GDN_HEAD_DIM, GDN_HEAD_DIM), F32)],
        compiler_params=_cparams("arbitrary"),
        name="gdn",
    )(r3(q), r3(k), r3(v), r3(z), r3(gb), norm_w)
    return out.reshape(t_rows, GDN_WIDTH)


def _s5_tables(lam_re, lam_im, log_step, b_re, b_im, c_re, c_im, n_lev):
    hp = lax.Precision.HIGHEST
    lr = jnp.minimum(lam_re.astype(F32), -1e-4)
    li = lam_im.astype(F32)
    dt = jnp.exp(log_step.astype(F32))[:, None]
    js = jnp.arange(S5_Q + 1, dtype=F32)[:, None, None]
    mag = jnp.exp(lr * dt * js)
    pr, pi = mag * jnp.cos(li * dt * js), mag * jnp.sin(li * dt * js)
    ar, ai = pr[1], pi[1]
    nr, ni = ar - 1.0, ai
    den = lr * lr + li * li
    cr, ci = (nr * lr + ni * li) / den, (ni * lr - nr * li) / den
    b_re, b_im = b_re.astype(F32), b_im.astype(F32)
    bbr = cr[..., None] * b_re - ci[..., None] * b_im
    bbi = cr[..., None] * b_im + ci[..., None] * b_re
    c_re, c_im = c_re.astype(F32), c_im.astype(F32)
    lbr = pr[..., None] * bbr - pi[..., None] * bbi
    lbi = pr[..., None] * bbi + pi[..., None] * bbr
    kern = (jnp.einsum('gon,jgni->gjoi', c_re, lbr[:S5_Q], precision=hp)
            - jnp.einsum('gon,jgni->gjoi', c_im, lbi[:S5_Q], precision=hp))
    s_idx = jnp.arange(S5_Q)[:, None]
    t_idx = jnp.arange(S5_Q)[None, :]
    lag = jnp.clip(t_idx - s_idx, 0, S5_Q - 1)
    toep = jnp.where((t_idx >= s_idx)[None, :, :, None, None], kern[:, lag], 0.0)
    toep = toep.transpose(0, 1, 4, 2, 3).reshape(S5_GROUPS, S5_Q * S5_GROUP, S5_Q * S5_GROUP)
    er = lbr[:S5_Q][::-1].transpose(1, 0, 3, 2).reshape(S5_GROUPS, S5_Q * S5_GROUP, S5_STATE)
    ei = lbi[:S5_Q][::-1].transpose(1, 0, 3, 2).reshape(S5_GROUPS, S5_Q * S5_GROUP, S5_STATE)
    e_mat = jnp.concatenate([er, ei], axis=-1)
    clr = c_re[None] * pr[1:, :, None, :] - c_im[None] * pi[1:, :, None, :]
    cli = c_re[None] * pi[1:, :, None, :] + c_im[None] * pr[1:, :, None, :]
    fr = clr.transpose(1, 3, 0, 2).reshape(S5_GROUPS, S5_STATE, S5_Q * S5_GROUP)
    fi = (-cli).transpose(1, 3, 0, 2).reshape(S5_GROUPS, S5_STATE, S5_Q * S5_GROUP)
    f_mat = jnp.concatenate([fr, fi], axis=1)
    a1, a2 = [], []
    cur_r, cur_i = pr[S5_Q], pi[S5_Q]
    for _ in range(n_lev):
        a1.append(jnp.concatenate([cur_r, cur_r], axis=-1))
        a2.append(jnp.concatenate([-cur_i, cur_i], axis=-1))
        cur_r, cur_i = cur_r * cur_r - cur_i * cur_i, 2.0 * cur_r * cur_i
    pad = [jnp.zeros_like(a1[0])] * (8 - n_lev % 8 if n_lev % 8 else 0)
    a1 = jnp.stack(a1 + pad, axis=1)
    a2 = jnp.stack(a2 + pad, axis=1)
    return toep.astype(BF16), e_mat.astype(BF16), f_mat.astype(BF16), a1, a2


def _s5_conv_kernel(u_ref, t_ref, e_ref, f_ref, a1_ref, a2_ref, y_ref, *, seg, n_lev):
    u = u_ref[0]
    y1 = jnp.dot(u, t_ref[0], preferred_element_type=F32)
    x = jnp.dot(u, e_ref[0], preferred_element_type=F32)
    rin = lax.broadcasted_iota(jnp.int32, x.shape, 0) % seg
    for lev in range(n_lev):
        s = 1 << lev
        xs = jnp.where(rin >= s, pltpu.roll(x, s, 0), 0.0)
        x = x + a1_ref[0, lev:lev + 1, :] * xs + a2_ref[0, lev:lev + 1, :] * pltpu.roll(xs, S5_STATE, 1)
    xp = jnp.where(rin >= 1, pltpu.roll(x, 1, 0), 0.0)
    y_ref[0] = y1 + _mm(xp, f_ref[0])


def _s5_conv(u_grp, toep, e_mat, f_mat, a1, a2, seg, n_lev):
    g, m, w = u_grp.shape
    kern = functools.partial(_s5_conv_kernel, seg=seg, n_lev=n_lev)
    per_g = lambda a: pl.BlockSpec((1,) + a.shape[1:], lambda i: (i, 0, 0))
    return pl.pallas_call(
        kern,
        grid=(g,),
        in_specs=[per_g(u_grp), per_g(toep), per_g(e_mat), per_g(f_mat), per_g(a1), per_g(a2)],
        out_specs=pl.BlockSpec((1, m, w), lambda i: (i, 0, 0)),
        out_shape=jax.ShapeDtypeStruct((g, m, w), F32),
        compiler_params=_cparams("parallel"),
        name="s5_conv",
    )(u_grp, toep, e_mat, f_mat, a1, a2)


def _s5_glu_kernel(y_ref, u_ref, d_ref, w_ref, b_ref, o_ref):
    y = y_ref[...] + d_ref[...] * u_ref[...]
    y = 0.5 * y * (1.0 + lax.erf(y * (2.0 ** -0.5)))
    ab = _mm(y, w_ref[...]) + b_ref[...]
    o_ref[...] = (ab[:, :S5_WIDTH] * jax.nn.sigmoid(ab[:, S5_WIDTH:])).astype(o_ref.dtype)


def _s5_glu(yconv, u, d_skip, glu_w, glu_b):
    t_rows = u.shape[0]
    tm = _row_tile(t_rows, 512)
    row = pl.BlockSpec((tm, S5_WIDTH), lambda i: (i, 0))
    full = lambda a: pl.BlockSpec(a.shape, lambda i: (0,) * a.ndim)
    return pl.pallas_call(
        _s5_glu_kernel,
        grid=(t_rows // tm,),
        in_specs=[row, row, full(d_skip), full(glu_w), full(glu_b)],
        out_specs=row,
        out_shape=jax.ShapeDtypeStruct((t_rows, S5_WIDTH), BF16),
        compiler_params=_cparams("parallel"),
        name="s5_glu",
    )(yconv, u, d_skip, glu_w, glu_b)


def _proj_res_kernel(*refs, n_in):
    x_ref, g_ref = refs[0], refs[1]
    ys = refs[2:2 + n_in]
    ws = refs[2 + n_in:2 + 2 * n_in]
    o_ref = refs[2 + 2 * n_in]
    acc = jnp.dot(ys[0][...], ws[0][...], preferred_element_type=F32)
    for y, w in zip(ys[1:], ws[1:]):
        acc = acc + jnp.dot(y[...], w[...], preferred_element_type=F32)
    o_ref[...] = x_ref[...] + g_ref[0] * acc


def _proj_res(x, gate, seq, ys, ws):
    t_rows = x.shape[0]
    tm = _row_tile(seq, 512)
    n_in = len(ys)
    full = lambda a: pl.BlockSpec(a.shape, lambda i: (0,) * a.ndim)
    return pl.pallas_call(
        functools.partial(_proj_res_kernel, n_in=n_in),
        grid=(t_rows // tm,),
        in_specs=([pl.BlockSpec((tm, D_MODEL), lambda i: (i, 0)), _mod_spec(tm, seq)]
                  + [pl.BlockSpec((tm, y.shape[1]), lambda i: (i, 0)) for y in ys]
                  + [full(w) for w in ws]),
        out_specs=pl.BlockSpec((tm, D_MODEL), lambda i: (i, 0)),
        out_shape=jax.ShapeDtypeStruct(x.shape, F32),
        compiler_params=_cparams("parallel"),
        name="proj_res",
    )(x, gate, *ys, *ws)


def _proj_res_t_kernel(x_ref, g_ref, yt_ref, w_ref, o_ref):
    acc = lax.dot_general(yt_ref[...], w_ref[...], (((0,), (0,)), ((), ())), preferred_element_type=F32)
    o_ref[...] = x_ref[...] + g_ref[0] * acc


def _proj_res_t(x, gate, seq, yt, w):
    t_rows = x.shape[0]
    tm = _row_tile(seq, 512)
    row = pl.BlockSpec((tm, D_MODEL), lambda i: (i, 0))
    return pl.pallas_call(
        _proj_res_t_kernel,
        grid=(t_rows // tm,),
        in_specs=[row, _mod_spec(tm, seq), pl.BlockSpec((yt.shape[0], tm), lambda i: (0, i)),
                  pl.BlockSpec(w.shape, lambda i: (0, 0))],
        out_specs=row,
        out_shape=jax.ShapeDtypeStruct(x.shape, F32),
        compiler_params=_cparams("parallel"),
        name="proj_res_t",
    )(x, gate, yt, w)


def _ffn_kernel(x_ref, nw_ref, sc_ref, sh_ref, g_ref, w1_ref, w3_ref, w2_ref, o_ref):
    x = x_ref[...]
    h = _norm_mod(x, nw_ref[...], sc_ref[0], sh_ref[0]).astype(BF16)
    a = jnp.dot(h, w1_ref[...], preferred_element_type=F32)
    b = jnp.dot(h, w3_ref[...], preferred_element_type=F32)
    f = jnp.dot((_silu(a) * b).astype(BF16), w2_ref[...], preferred_element_type=F32)
    o_ref[...] = x + g_ref[0] * f


def _ffn(x, seq, nw, sc, sh, gate, w1, w3, w2):
    t_rows = x.shape[0]
    tm = _row_tile(seq, 256)
    row = pl.BlockSpec((tm, D_MODEL), lambda i: (i, 0))
    once = lambda a: pl.BlockSpec(a.shape, lambda i: (0,) * a.ndim, pipeline_mode=pl.Buffered(1))
    return pl.pallas_call(
        _ffn_kernel,
        grid=(t_rows // tm,),
        in_specs=[row, once(nw), _mod_spec(tm, seq), _mod_spec(tm, seq), _mod_spec(tm, seq),
                  once(w1), once(w3), once(w2)],
        out_specs=row,
        out_shape=jax.ShapeDtypeStruct(x.shape, F32),
        compiler_params=_cparams("parallel"),
        name="ffn",
    )(x, nw, sc, sh, gate, w1, w3, w2)


LOG2E = 1.4426950408889634


def _odd_qkv_kernel(x_ref, nw_ref, sc_ref, sh_ref, w_ref, wvt_ref, qw_ref, kw_ref, q_ref, k_ref, vt_ref):
    h = _norm_mod(x_ref[...], nw_ref[...], sc_ref[0], sh_ref[0]).astype(BF16)
    proj = jnp.dot(h, w_ref[...], preferred_element_type=F32)
    r = lax.broadcasted_iota(jnp.int32, (LANES, LANES), 0) // DIFF_HEAD_DIM
    c = lax.broadcasted_iota(jnp.int32, (LANES, LANES), 1) // DIFF_HEAD_DIM
    grp = (r == c).astype(BF16)
    q_scale = DIFF_HEAD_DIM ** -0.5 * LOG2E
    for base, w_vec, scale, out in ((0, qw_ref, q_scale, q_ref), (D_MODEL, kw_ref, 1.0, k_ref)):
        for t in range(D_MODEL // LANES):
            xt = proj[:, base + t * LANES:base + (t + 1) * LANES]
            ss = sum(jnp.dot(p, grp, preferred_element_type=F32) for p in _split3(xt * xt))
            y = xt * lax.rsqrt(ss * (1.0 / DIFF_HEAD_DIM) + EPS) * w_vec[...]
            out[:, t * LANES:(t + 1) * LANES] = (y * scale).astype(out.dtype)
    vt_ref[...] = _mm_nt(wvt_ref[...], h).astype(vt_ref.dtype)


def _odd_qkv(x, seq, nw, sc, sh, w_qk, w_vt, qw, kw):
    t_rows = x.shape[0]
    tm = _row_tile(seq, 512)
    row = pl.BlockSpec((tm, D_MODEL), lambda i: (i, 0))
    full = lambda a: pl.BlockSpec(a.shape, lambda i: (0,) * a.ndim)
    out = jax.ShapeDtypeStruct((t_rows, D_MODEL), BF16)
    return pl.pallas_call(
        _odd_qkv_kernel,
        grid=(t_rows // tm,),
        in_specs=[row, full(nw), _mod_spec(tm, seq), _mod_spec(tm, seq), full(w_qk), full(w_vt), full(qw),
                  full(kw)],
        out_specs=[row, row, pl.BlockSpec((D_MODEL, tm), lambda i: (0, i))],
        out_shape=[out, out, jax.ShapeDtypeStruct((D_MODEL, t_rows), BF16)],
        compiler_params=_cparams("parallel"),
        name="odd_qkv",
    )(x, nw, sc, sh, w_qk, w_vt, qw, kw)


def _attn_kernel(bound_ref, q_ref, k_ref, vt_ref, lam_ref, sw_ref, o_ref, acc0, acc1, pp_scr, l_scr,
                 *, tq, tk, out_scale):
    i = pl.program_id(2)
    q = q_ref[...]
    lane = lax.broadcasted_iota(jnp.int32, q.shape, 1)
    zero = jnp.zeros_like(q)
    qs = (jnp.where(lane < DIFF_HEAD_DIM, q, zero), jnp.where(lane >= DIFF_HEAD_DIM, q, zero))
    accs = (acc0, acc1)
    n_sub = tq // LANES
    n_full = i * (tq // tk)

    def load_kv(j):
        k0 = pl.multiple_of(j * tk, tk)
        return k0, k_ref[pl.ds(k0, tk), :], vt_ref[:, pl.ds(k0, tk)]

    def scores(kb, t, c, k0, masked):
        s = lax.dot_general(kb, qs[t][c * LANES:(c + 1) * LANES, :], (((1,), (1,)), ((), ())),
                            preferred_element_type=F32)
        if masked:
            kc = (k0 + lax.broadcasted_iota(jnp.int32, (tk, LANES), 0)) // CHUNK
            qc = (i * tq + c * LANES + lax.broadcasted_iota(jnp.int32, (tk, LANES), 1)) // CHUNK
            s = jnp.where(kc <= qc, s, -1e30)
        return s

    def finish(l0, l1):
        lam = lam_ref[0:1, 0:1]
        o = acc0[...] / l0 - lam * (acc1[...] / l1)
        o = o * lax.rsqrt(jnp.mean(o * o, axis=0, keepdims=True) + EPS) * sw_ref[...] * out_scale
        o_ref[...] = o.astype(o_ref.dtype)

    acc0[...] = jnp.zeros_like(acc0)
    acc1[...] = jnp.zeros_like(acc1)

    @pl.when(bound_ref[0] == 1)
    def _():
        l_scr[...] = jnp.zeros_like(l_scr)
        pp_scr[1] = jnp.zeros(pp_scr.shape[1:], BF16)

        def values(j):
            vb = vt_ref[:, pl.ds(pl.multiple_of(j * tk, tk), tk)]
            for t in range(2):
                accs[t][...] = accs[t][...] + jnp.dot(vb, pp_scr[j % 2, t], preferred_element_type=F32)

        def step(j, masked):
            values(jnp.maximum(j - 1, 0) + (j == 0).astype(jnp.int32))
            k0, kb, _ = load_kv(j)
            for t in range(2):
                for c in range(n_sub):
                    p = jnp.exp2(scores(kb, t, c, k0, masked))
                    cs = slice(c * LANES, (c + 1) * LANES)
                    l_scr[t, :, cs] = l_scr[t, :, cs] + jnp.sum(p.reshape(tk // 8, 8, LANES), axis=0)
                    pp_scr[j % 2, t, :, cs] = p.astype(BF16)

        def body(j, carry):
            step(j, False)
            return carry

        lax.fori_loop(0, n_full, body, 0)
        n_mask = tq // tk
        for jj in range(n_mask):
            step(n_full + jj, True)
        values(n_full + n_mask - 1)
        finish(jnp.sum(l_scr[0], axis=0, keepdims=True), jnp.sum(l_scr[1], axis=0, keepdims=True))

    @pl.when(bound_ref[0] != 1)
    def _():
        neg = jnp.full((1, tq), -1e30, F32)
        zer = jnp.zeros((1, tq), F32)

        def block(j, carry, masked):
            k0, kb, vb = load_kv(j)
            out, alphas = [], []
            for t in range(2):
                m, l = carry[2 * t], carry[2 * t + 1]
                m_parts, l_parts, a_parts = [], [], []
                for c in range(n_sub):
                    cs = slice(c * LANES, (c + 1) * LANES)
                    s = scores(kb, t, c, k0, masked)
                    m_new = jnp.maximum(m[:, cs], jnp.max(s, axis=0, keepdims=True))
                    p = jnp.exp2(s - m_new)
                    alpha = jnp.exp2(m[:, cs] - m_new)
                    l_parts.append(alpha * l[:, cs] + jnp.sum(p, axis=0, keepdims=True))
                    pp_scr[0, t, :, cs] = p.astype(BF16)
                    m_parts.append(m_new)
                    a_parts.append(alpha)
                out += [jnp.concatenate(m_parts, axis=1), jnp.concatenate(l_parts, axis=1)]
                alphas.append(jnp.concatenate(a_parts, axis=1))
            for t in range(2):
                accs[t][...] = alphas[t] * accs[t][...] + jnp.dot(vb, pp_scr[0, t],
                                                                   preferred_element_type=F32)
            return tuple(out)

        carry = lax.fori_loop(0, n_full, lambda j, c: block(j, c, False), (neg, zer, neg, zer))
        for jj in range(tq // tk):
            carry = block(n_full + jj, carry, True)
        finish(carry[1], carry[3])


SCORE_BOUND_LOG2 = 64.0


def _attention(q, k, vt, score_bound, lam_vec, subln_col, batch, seq, out_scale):
    tq = _row_tile(seq, 512)
    tk = _row_tile(tq, 256)
    nq = seq // tq
    kern = functools.partial(_attn_kernel, tq=tq, tk=tk, out_scale=out_scale)
    full = lambda a: pl.BlockSpec(a.shape, lambda b, h, i: (0,) * a.ndim)
    bounded = (score_bound <= SCORE_BOUND_LOG2).astype(jnp.int32).reshape(1)
    return pl.pallas_call(
        kern,
        grid=(batch, DIFF_HEADS, nq),
        in_specs=[pl.BlockSpec(memory_space=pltpu.SMEM),
                  pl.BlockSpec((tq, LANES), lambda b, h, i: (b * nq + i, h)),
                  pl.BlockSpec((seq, LANES), lambda b, h, i: (b, h)),
                  pl.BlockSpec((LANES, seq), lambda b, h, i: (h, b)),
                  full(lam_vec), full(subln_col)],
        out_specs=pl.BlockSpec((LANES, tq), lambda b, h, i: (h, b * nq + i)),
        out_shape=jax.ShapeDtypeStruct(vt.shape, BF16),
        scratch_shapes=[pltpu.VMEM((LANES, tq), F32), pltpu.VMEM((LANES, tq), F32),
                        pltpu.VMEM((2, 2, tk, tq), BF16), pltpu.VMEM((2, 8, tq), F32)],
        compiler_params=_cparams("parallel", "parallel", "arbitrary"),
        name="diff_attn",
    )(bounded, q, k, vt, lam_vec, subln_col)


def _router_kernel(x_ref, nw_ref, sc_ref, sh_ref, whi_ref, wlo_ref, h_ref, idx_ref, gate_ref):
    h = _norm_mod(x_ref[...], nw_ref[...], sc_ref[0], sh_ref[0])
    h_ref[...] = h
    h_hi = h.astype(BF16)
    h_lo = (h - h_hi.astype(F32)).astype(BF16)
    logits = (jnp.dot(h_hi, whi_ref[...], preferred_element_type=F32)
              + jnp.dot(h_hi, wlo_ref[...], preferred_element_type=F32)
              + jnp.dot(h_lo, whi_ref[...], preferred_element_type=F32))
    lane = lax.broadcasted_iota(jnp.int32, logits.shape, 1)
    lane_f = lane.astype(F32)
    logits = jnp.where(lane < N_EXPERTS, logits, -jnp.inf)
    m1 = jnp.max(logits, axis=-1, keepdims=True)
    i1 = jnp.min(jnp.where(logits == m1, lane_f, float(LANES)), axis=-1, keepdims=True)
    rest = jnp.where(lane_f == i1, -jnp.inf, logits)
    m2 = jnp.max(rest, axis=-1, keepdims=True)
    i2 = jnp.min(jnp.where(rest == m2, lane_f, float(LANES)), axis=-1, keepdims=True)
    e = jnp.exp(m2 - m1)
    g1 = 1.0 / (1.0 + e)
    idx_ref[...] = jnp.where(lane == 0, i1, jnp.where(lane == 1, i2, 0.0)).astype(jnp.int32)
    gate_ref[...] = jnp.where(lane == 0, g1, jnp.where(lane == 1, e * g1, 0.0))


def _router(x, seq, nw, sc, sh, w_hi, w_lo):
    t_rows = x.shape[0]
    tm = _row_tile(seq, 512)
    row = lambda w: pl.BlockSpec((tm, w), lambda i: (i, 0))
    full = lambda a: pl.BlockSpec(a.shape, lambda i: (0,) * a.ndim)
    return pl.pallas_call(
        _router_kernel,
        grid=(t_rows // tm,),
        in_specs=[row(D_MODEL), full(nw), _mod_spec(tm, seq), _mod_spec(tm, seq), full(w_hi), full(w_lo)],
        out_specs=[row(D_MODEL), row(LANES), row(LANES)],
        out_shape=[jax.ShapeDtypeStruct((t_rows, D_MODEL), F32),
                   jax.ShapeDtypeStruct((t_rows, LANES), jnp.int32),
                   jax.ShapeDtypeStruct((t_rows, LANES), F32)],
        compiler_params=_cparams("parallel"),
        name="router",
    )(x, nw, sc, sh, w_hi, w_lo)


GATHER_ROWS = 256


def _row_copy(src_hbm, dst_ref, sem, src_row, dst_row):
    return pltpu.make_async_copy(src_hbm.at[pl.ds(src_row, 1), :], dst_ref.at[pl.ds(dst_row, 1), :], sem)


def _gather_kernel(idx_ref, src_hbm, o_ref, sem):
    def start(r, _):
        _row_copy(src_hbm, o_ref, sem, idx_ref[0, 0, r], r).start()
        return 0

    lax.fori_loop(0, GATHER_ROWS, start, 0, unroll=8)
    pltpu.make_async_copy(src_hbm.at[pl.ds(0, GATHER_ROWS), :], o_ref, sem).wait()


def _gather_rows(src, idx):
    n = idx.shape[0]
    width = src.shape[1]
    return pl.pallas_call(
        _gather_kernel,
        grid=(n // GATHER_ROWS,),
        in_specs=[pl.BlockSpec((1, 1, GATHER_ROWS), lambda i: (i, 0, 0), memory_space=pltpu.SMEM),
                  pl.BlockSpec(memory_space=pl.ANY)],
        out_specs=pl.BlockSpec((GATHER_ROWS, width), lambda i: (i, 0)),
        scratch_shapes=[pltpu.SemaphoreType.DMA(())],
        out_shape=jax.ShapeDtypeStruct((n, width), src.dtype),
        compiler_params=_cparams("arbitrary"),
        name="gather_rows",
    )(idx.reshape(n // GATHER_ROWS, 1, GATHER_ROWS), src)


def _expert_kernel(te_ref, nt_ref, x_ref, w1_ref, w3_ref, w2_ref, o_ref, acc_scr):
    i = pl.program_id(0)
    f = pl.program_id(1)

    @pl.when(i < nt_ref[0])
    def _():
        h = x_ref[...].astype(BF16)
        a = jnp.dot(h, w1_ref[0], preferred_element_type=F32)
        b = jnp.dot(h, w3_ref[0], preferred_element_type=F32)
        part = jnp.dot((_silu(a) * b).astype(BF16), w2_ref[0], preferred_element_type=F32)

        @pl.when(f == 0)
        def _():
            acc_scr[...] = part

        @pl.when(f > 0)
        def _():
            acc_scr[...] = acc_scr[...] + part

    @pl.when(f == pl.num_programs(1) - 1)
    def _():
        o_ref[...] = jnp.where(i < nt_ref[0], acc_scr[...], 0.0)


EXPERT_TM = 512
EXPERT_TF = D_FF // 2


def _expert_ffn(x_perm, tile_expert, n_tiles_used, w13, w2):
    p_rows = x_perm.shape[0]
    tm, tf = EXPERT_TM, EXPERT_TF
    nf = D_FF // tf
    return pl.pallas_call(
        _expert_kernel,
        grid_spec=pltpu.PrefetchScalarGridSpec(
            num_scalar_prefetch=2,
            grid=(p_rows // tm, nf),
            in_specs=[pl.BlockSpec((tm, D_MODEL), lambda i, f, te, nt: (i, 0)),
                      pl.BlockSpec((1, D_MODEL, tf), lambda i, f, te, nt: (te[i], 0, f)),
                      pl.BlockSpec((1, D_MODEL, tf), lambda i, f, te, nt: (te[i], 0, f + nf)),
                      pl.BlockSpec((1, tf, D_MODEL), lambda i, f, te, nt: (te[i], f, 0))],
            out_specs=pl.BlockSpec((tm, D_MODEL), lambda i, f, te, nt: (i, 0)),
            scratch_shapes=[pltpu.VMEM((tm, D_MODEL), F32)],
        ),
        out_shape=jax.ShapeDtypeStruct((p_rows, D_MODEL), F32),
        compiler_params=_cparams("parallel", "arbitrary"),
        name="expert_ffn",
    )(tile_expert, n_tiles_used, x_perm, w13, w13, w2)


def _combine_kernel(x_ref, g_ref, gate_ref, y0_ref, y1_ref, o_ref):
    gates = gate_ref[...]
    f = gates[:, 0:1] * y0_ref[...] + gates[:, 1:2] * y1_ref[...]
    o_ref[...] = x_ref[...] + g_ref[0] * f


def _combine(x, gate_mod, seq, gates, y_tok):
    t_rows = x.shape[0]
    tm = _row_tile(seq, 512)
    nt = t_rows // tm
    row = pl.BlockSpec((tm, D_MODEL), lambda i: (i, 0))
    return pl.pallas_call(
        _combine_kernel,
        grid=(nt,),
        in_specs=[row, _mod_spec(tm, seq), pl.BlockSpec((tm, LANES), lambda i: (i, 0)),
                  row, pl.BlockSpec((tm, D_MODEL), lambda i: (i + nt, 0))],
        out_specs=row,
        out_shape=jax.ShapeDtypeStruct(x.shape, F32),
        compiler_params=_cparams("parallel"),
        name="moe_combine",
    )(x, gate_mod, gates, y_tok, y_tok)


def _dispatch_plan(expert_idx, t_rows):
    tm = EXPERT_TM
    flat = expert_idx.reshape(-1)
    onehot = (flat[:, None] == jnp.arange(N_EXPERTS)[None, :]).astype(jnp.int32)
    rank = jnp.take_along_axis(jnp.cumsum(onehot, axis=0) - onehot, flat[:, None], axis=1)[:, 0]
    counts = jnp.sum(onehot, axis=0)
    tiles_per = (counts + tm - 1) // tm
    tile_end = jnp.cumsum(tiles_per)
    offs = (tile_end - tiles_per) * tm
    pos = offs[flat] + rank
    n_tiles = (2 * t_rows) // tm + N_EXPERTS
    p_rows = n_tiles * tm
    tok_of_slot = jnp.zeros((p_rows,), jnp.int32).at[pos].set(jnp.arange(2 * t_rows, dtype=jnp.int32) // 2)
    tile_expert = jnp.minimum(
        jnp.sum(jnp.arange(n_tiles, dtype=jnp.int32)[:, None] >= tile_end[None, :], axis=1), N_EXPERTS - 1
    ).astype(jnp.int32)
    n_used = tile_end[-1:].astype(jnp.int32)
    pos2 = pos.reshape(t_rows, 2)
    gather_back = jnp.concatenate([pos2[:, 0], pos2[:, 1]]).astype(jnp.int32)
    return tok_of_slot, tile_expert, n_used, gather_back


def _mods(mod, layer, batch):
    m = mod[layer, :batch].reshape(batch, 6, 1, D_MODEL)
    return [m[:, j] for j in range(6)]


def _even_layer(x, batch, seq, mods, nw_mix, nw_ffn, w_in, conv_w, a_log, dt_bias, gdn_norm_w,
                lam_re, lam_im, log_step, b_re, b_im, c_re, c_im, d_skip, glu_w, glu_b, w_out, w13, w2):
    sh1, sc1, g1, sh2, sc2, g2 = mods
    t_rows = batch * seq
    qkvz, rest = w_in[:, :4 * GDN_WIDTH], w_in[:, 4 * GDN_WIDTH:]
    w_ba, w_u = rest[:, :2 * GDN_HEADS], rest[:, 2 * GDN_HEADS:]
    w_cat = jnp.concatenate(
        [qkvz, w_u, w_ba, jnp.zeros((D_MODEL, LANES - 2 * GDN_HEADS), w_in.dtype)], axis=1).astype(BF16)
    gate_params = jnp.zeros((8, LANES), F32)
    gate_params = gate_params.at[0, GDN_HEADS:2 * GDN_HEADS].set(a_log.astype(F32))
    gate_params = gate_params.at[1, GDN_HEADS:2 * GDN_HEADS].set(dt_bias.astype(F32))
    q, k, v, z, u, gb = _even_in(x, seq, nw_mix, sc1, sh1, w_cat, conv_w.astype(F32), gate_params)
    y_a = _gdn(q, k, v, z, gb, gdn_norm_w.reshape(1, GDN_HEAD_DIM).astype(F32), batch, seq)

    seg = seq // S5_Q
    n_lev = max(int(math.ceil(math.log2(seg))), 0)
    toep, e_mat, f_mat, a1, a2 = _s5_tables(lam_re, lam_im, log_step, b_re, b_im, c_re, c_im, max(n_lev, 1))
    m_rows = t_rows // S5_Q
    u_grp = (u.reshape(m_rows, S5_Q, S5_GROUPS, S5_GROUP).transpose(2, 0, 1, 3)
             .reshape(S5_GROUPS, m_rows, S5_Q * S5_GROUP).astype(BF16))
    yg = _s5_conv(u_grp, toep, e_mat, f_mat, a1, a2, seg, n_lev)
    yconv = (yg.reshape(S5_GROUPS, m_rows, S5_Q, S5_GROUP).transpose(1, 2, 0, 3)
             .reshape(t_rows, S5_WIDTH))
    y_b = _s5_glu(yconv, u, d_skip.reshape(1, S5_WIDTH).astype(F32), glu_w.astype(BF16),
                  glu_b.reshape(1, 2 * S5_WIDTH).astype(F32))
    wo = w_out.astype(BF16)
    x = _proj_res(x, g1, seq, [y_a, y_b], [wo[:GDN_WIDTH], wo[GDN_WIDTH:]])
    w13b = w13.astype(BF16)
    return _ffn(x, seq, nw_ffn, sc2, sh2, g2, w13b[:, :D_FF], w13b[:, D_FF:], w2.astype(BF16))


def _odd_layer(x, batch, seq, mods, nw_mix, nw_ffn, w_qkv, q_norm_w, k_norm_w, lq1, lk1, lq2, lk2,
               subln_w, w_out, router_w, w13, w2, lambda_init):
    sh1, sc1, g1, sh2, sc2, g2 = mods
    t_rows = batch * seq
    tile2 = lambda w: jnp.tile(w.astype(F32), LANES // DIFF_HEAD_DIM).reshape(1, LANES)
    w_qkv_b = w_qkv.astype(BF16)
    q, k, vt = _odd_qkv(x, seq, nw_mix, sc1, sh1, w_qkv_b[:, :2 * D_MODEL], w_qkv_b[:, 2 * D_MODEL:].T,
                        tile2(q_norm_w), tile2(k_norm_w))
    lam = (jnp.exp(jnp.sum(lq1.astype(F32) * lk1.astype(F32)))
           - jnp.exp(jnp.sum(lq2.astype(F32) * lk2.astype(F32))) + lambda_init)
    lam_vec = jnp.full((8, LANES), lam, F32)
    score_bound = (1.02 * LOG2E * DIFF_HEAD_DIM ** 0.5
                   * jnp.max(jnp.abs(q_norm_w.astype(F32))) * jnp.max(jnp.abs(k_norm_w.astype(F32))))
    ot = _attention(q, k, vt, score_bound, lam_vec, subln_w.reshape(LANES, 1).astype(F32), batch, seq,
                    1.0 - lambda_init)
    x = _proj_res_t(x, g1, seq, ot, w_out.astype(BF16))

    rw = jnp.zeros((D_MODEL, LANES), F32).at[:, :N_EXPERTS].set(router_w.astype(F32))
    rw_hi = rw.astype(BF16)
    rw_lo = (rw - rw_hi.astype(F32)).astype(BF16)
    h, idx, gates = _router(x, seq, nw_ffn, sc2, sh2, rw_hi, rw_lo)
    tok_of_slot, tile_expert, n_used, gather_back = _dispatch_plan(idx[:, :2], t_rows)
    h_perm = _gather_rows(h, tok_of_slot)
    y_perm = _expert_ffn(h_perm, tile_expert, n_used, w13.astype(BF16), w2.astype(BF16))
    y_tok = _gather_rows(y_perm, gather_back)
    return _combine(x, g2, seq, gates, y_tok)


def kernel(x, c, ada_w, ada_b, norm_mix_w, norm_ffn_w, even_w_in, even_conv_w, even_a_log, even_dt_bias, even_gdn_norm_w, even_lam_re, even_lam_im, even_log_step, even_b_re, even_b_im, even_c_re, even_c_im, even_d_skip, even_glu_w, even_glu_b, even_w_out, even_ffn_w13, even_ffn_w2, odd_w_qkv, odd_q_norm_w, odd_k_norm_w, odd_lambda_q1, odd_lambda_k1, odd_lambda_q2, odd_lambda_k2, odd_subln_w, odd_w_out, odd_router_w, odd_expert_w13, odd_expert_w2):
    batch, seq, d = x.shape
    assert d == D_MODEL and seq % CHUNK == 0 and seq % S5_Q == 0
    c_pad = jnp.zeros((8, d), F32).at[:batch].set(c.astype(F32))
    mod = _adaln(c_pad, ada_w.astype(BF16), ada_b.astype(F32))
    xf = x.astype(F32).reshape(batch * seq, d)
    for layer in range(DEPTH):
        i = layer // 2
        mods = _mods(mod, layer, batch)
        nw_mix = norm_mix_w[layer].reshape(1, d).astype(F32)
        nw_ffn = norm_ffn_w[layer].reshape(1, d).astype(F32)
        if layer % 2 == 0:
            xf = _even_layer(xf, batch, seq, mods, nw_mix, nw_ffn, even_w_in[i], even_conv_w[i], even_a_log[i],
                             even_dt_bias[i], even_gdn_norm_w[i], even_lam_re[i], even_lam_im[i],
                             even_log_step[i], even_b_re[i], even_b_im[i], even_c_re[i], even_c_im[i],
                             even_d_skip[i], even_glu_w[i], even_glu_b[i], even_w_out[i],
                             even_ffn_w13[i], even_ffn_w2[i])
        else:
            lambda_init = 0.8 - 0.6 * math.exp(-0.3 * layer)
            xf = _odd_layer(xf, batch, seq, mods, nw_mix, nw_ffn, odd_w_qkv[i], odd_q_norm_w[i], odd_k_norm_w[i],
                            odd_lambda_q1[i], odd_lambda_k1[i], odd_lambda_q2[i], odd_lambda_k2[i],
                            odd_subln_w[i], odd_w_out[i], odd_router_w[i], odd_expert_w13[i],
                            odd_expert_w2[i], lambda_init)
    return xf.reshape(batch, seq, d).astype(x.dtype)
```

```python
import functools
import math

import jax
import jax.numpy as jnp
from jax import lax
from jax.experimental import pallas as pl
from jax.experimental.pallas import tpu as pltpu

F32 = jnp.float32
BF16 = jnp.bfloat16

D_MODEL = 1024
DEPTH = 4
CHUNK = 64
EPS = 1e-6
GDN_HEAD_DIM = 128
GDN_WIDTH = D_MODEL // 2
GDN_HEADS = GDN_WIDTH // GDN_HEAD_DIM
CONV_K = 4
QKV_W = 3 * GDN_WIDTH
S5_WIDTH = D_MODEL - GDN_WIDTH
S5_GROUP = 16
S5_GROUPS = S5_WIDTH // S5_GROUP
S5_STATE = 64
S5_Q = 16
DIFF_HEAD_DIM = 64
DIFF_HEADS = D_MODEL // (2 * DIFF_HEAD_DIM)
D_FF = ((8 * D_MODEL // 3 + 127) // 128) * 128
N_EXPERTS = 8
LANES = 128
EVEN_IN_COLS = QKV_W + GDN_WIDTH + S5_WIDTH + LANES
VMEM_LIMIT = 56 * 1024 * 1024


def _cparams(*sem):
    return pltpu.CompilerParams(dimension_semantics=sem, vmem_limit_bytes=VMEM_LIMIT)


def _mm(a, b):
    return jnp.dot(a.astype(BF16), b.astype(BF16), preferred_element_type=F32)


def _mm_nt(a, b):
    return lax.dot_general(a.astype(BF16), b.astype(BF16), (((1,), (1,)), ((), ())),
                           preferred_element_type=F32)


def _mm_tn(a, b):
    return lax.dot_general(a.astype(BF16), b.astype(BF16), (((0,), (0,)), ((), ())),
                           preferred_element_type=F32)


def _mm_hi(a, b):
    a_hi = a.astype(BF16)
    a_lo = (a - a_hi.astype(F32)).astype(BF16)
    b_hi = b.astype(BF16)
    b_lo = (b - b_hi.astype(F32)).astype(BF16)
    return (jnp.dot(a_hi, b_hi, preferred_element_type=F32)
            + (jnp.dot(a_hi, b_lo, preferred_element_type=F32) + jnp.dot(a_lo, b_hi, preferred_element_type=F32)))


def _split3(x):
    hi = x.astype(BF16)
    r = x - hi.astype(F32)
    mid = r.astype(BF16)
    lo = (r - mid.astype(F32)).astype(BF16)
    return hi, mid, lo


def _norm_mod(x, nw, sc, sh):
    ms = jnp.mean(x * x, axis=-1, keepdims=True)
    return (x * lax.rsqrt(ms + EPS) * nw) * (1.0 + sc) + sh


def _silu(x):
    return x * jax.nn.sigmoid(x)


def _row_tile(n, pref):
    t = min(pref, n)
    while n % t:
        t //= 2
    return t


def _adaln_kernel(c_ref, w_ref, b_ref, o_ref):
    c = c_ref[...]
    o_ref[0] = _mm(_silu(c), w_ref[0]) + b_ref[0]


def _adaln(c_pad, ada_w, ada_b):
    depth, d, n = ada_w.shape
    tn = 1536
    rows = c_pad.shape[0]
    return pl.pallas_call(
        _adaln_kernel,
        grid=(depth, n // tn),
        in_specs=[pl.BlockSpec((rows, d), lambda l, j: (0, 0)),
                  pl.BlockSpec((1, d, tn), lambda l, j: (l, 0, j)),
                  pl.BlockSpec((1, 1, tn), lambda l, j: (l, 0, j))],
        out_specs=pl.BlockSpec((1, rows, tn), lambda l, j: (l, 0, j)),
        out_shape=jax.ShapeDtypeStruct((depth, rows, n), F32),
        compiler_params=_cparams("parallel", "parallel"),
        name="adaln",
    )(c_pad, ada_w, ada_b.reshape(depth, 1, n))


def _mod_spec(tm, seq):
    return pl.BlockSpec((1, 1, D_MODEL), lambda i, *_: ((i * tm) // seq, 0, 0))


def _even_in_kernel(x_ref, xh_ref, nw_ref, sc_ref, sh_ref, w_ref, cw_ref, gp_ref,
                    q_ref, k_ref, v_ref, z_ref, u_ref, gb_ref, pre_scr, *, tm, seq):
    i = pl.program_id(0)
    nw = nw_ref[...]
    sc = sc_ref[0]
    sh = sh_ref[0]
    h = _norm_mod(x_ref[...], nw, sc, sh).astype(BF16)
    proj = jnp.dot(h, w_ref[...], preferred_element_type=F32)
    hh = _norm_mod(xh_ref[...], nw, sc, sh).astype(BF16)
    preh = jnp.dot(hh, w_ref[:, :QKV_W], preferred_element_type=F32)
    preh = jnp.where((i * tm) % seq == 0, 0.0, preh)
    pre = proj[:, :QKV_W]
    pre_scr[0:8, :] = preh
    pre_scr[8:8 + tm, :] = pre
    acc = pre * cw_ref[CONV_K - 1:CONV_K, :]
    for j in range(CONV_K - 1):
        acc = acc + pre_scr[pl.ds(8 - (CONV_K - 1) + j, tm), :] * cw_ref[j:j + 1, :]
    qkv = _silu(acc)
    for hd in range(GDN_HEADS):
        lo = hd * GDN_HEAD_DIM
        qh = qkv[:, lo:lo + GDN_HEAD_DIM]
        kh = qkv[:, GDN_WIDTH + lo:GDN_WIDTH + lo + GDN_HEAD_DIM]
        q_ref[:, lo:lo + GDN_HEAD_DIM] = (
            qh * lax.rsqrt(jnp.sum(qh * qh, axis=-1, keepdims=True) + EPS) * GDN_HEAD_DIM ** -0.5)
        k_ref[:, lo:lo + GDN_HEAD_DIM] = kh * lax.rsqrt(jnp.sum(kh * kh, axis=-1, keepdims=True) + EPS)
    v_ref[...] = qkv[:, 2 * GDN_WIDTH:]
    z_ref[...] = proj[:, QKV_W:QKV_W + GDN_WIDTH]
    u_ref[...] = proj[:, QKV_W + GDN_WIDTH:QKV_W + GDN_WIDTH + S5_WIDTH]
    ba = proj[:, QKV_W + GDN_WIDTH + S5_WIDTH:]
    lane = lax.broadcasted_iota(jnp.int32, ba.shape, 1)
    t = ba + gp_ref[1:2, :]
    softplus = jnp.maximum(t, 0.0) + jnp.log1p(jnp.exp(-jnp.abs(t)))
    g = -jnp.exp(gp_ref[0:1, :]) * softplus
    gb_ref[...] = jnp.where(lane < GDN_HEADS, jax.nn.sigmoid(ba), g)


def _even_in(x, seq, nw, sc, sh, w_cat, conv_w, gate_params):
    t_rows = x.shape[0]
    tm = _row_tile(seq, 512)
    kern = functools.partial(_even_in_kernel, tm=tm, seq=seq)
    row = lambda w: pl.BlockSpec((tm, w), lambda i: (i, 0))
    full = lambda a: pl.BlockSpec(a.shape, lambda i: (0,) * a.ndim)
    outs = [jax.ShapeDtypeStruct((t_rows, GDN_WIDTH), F32)] * 5 + [jax.ShapeDtypeStruct((t_rows, LANES), F32)]
    return pl.pallas_call(
        kern,
        grid=(t_rows // tm,),
        in_specs=[row(D_MODEL),
                  pl.BlockSpec((8, D_MODEL), lambda i: (jnp.maximum(i * (tm // 8) - 1, 0), 0)),
                  full(nw), _mod_spec(tm, seq), _mod_spec(tm, seq), full(w_cat), full(conv_w),
                  full(gate_params)],
        out_specs=[row(GDN_WIDTH)] * 5 + [row(LANES)],
        out_shape=outs,
        scratch_shapes=[pltpu.VMEM((tm + 8, QKV_W), F32)],
        compiler_params=_cparams("parallel"),
        name="even_in",
    )(x, x, nw, sc, sh, w_cat, conv_w, gate_params)


def _gdn_kernel(q_ref, k_ref, v_ref, z_ref, gb_ref, nw_ref, ya_ref, state_scr, *, batch):
    @pl.when(pl.program_id(0) == 0)
    def _():
        state_scr[...] = jnp.zeros_like(state_scr)

    r64 = lax.broadcasted_iota(jnp.int32, (CHUNK, CHUNK), 0)
    c64 = lax.broadcasted_iota(jnp.int32, (CHUNK, CHUNK), 1)
    causal = r64 >= c64
    strict = r64 > c64
    tri = causal.astype(BF16)
    sel = (lax.broadcasted_iota(jnp.int32, (8, LANES), 1)
           == lax.broadcasted_iota(jnp.int32, (8, LANES), 0) + GDN_HEADS).astype(BF16)
    nw = nw_ref[...]
    units = [(b, hd) for b in range(batch) for hd in range(GDN_HEADS)]
    gbs = [gb_ref[b] for b in range(batch)]
    gcs = [sum(jnp.dot(tri, p, preferred_element_type=F32) for p in _split3(gb)) for gb in gbs]
    gcts = [sum(lax.dot_general(sel, p, (((1,), (1,)), ((), ())), preferred_element_type=F32)
                for p in _split3(gc)) for gc in gcs]

    qs, ks, gcols, egs, kbs = [], [], [], [], []
    for b, hd in units:
        sl = slice(hd * GDN_HEAD_DIM, (hd + 1) * GDN_HEAD_DIM)
        qs.append(q_ref[b, :, sl])
        ks.append(k_ref[b, :, sl])
        gcols.append(gcs[b][:, GDN_HEADS + hd:GDN_HEADS + hd + 1])
        egs.append(jnp.exp(gcols[-1]))
        kbs.append(ks[-1] * gbs[b][:, hd:hd + 1])
    kks = [_mm_nt(kb, kh) for kb, kh in zip(kbs, ks)]
    qks = [_mm_nt(qh, kh) for qh, kh in zip(qs, ks)]
    p_mats, sols = [], []
    for u, (b, hd) in enumerate(units):
        sl = slice(hd * GDN_HEAD_DIM, (hd + 1) * GDN_HEAD_DIM)
        decay = jnp.exp(jnp.where(causal, gcols[u] - gcts[b][hd:hd + 1, :], -1e30))
        p_mats.append(jnp.where(strict, -kks[u] * decay, 0.0))
        qks[u] = jnp.where(causal, qks[u] * decay, 0.0)
        sols.append(jnp.concatenate([v_ref[b, :, sl] * gbs[b][:, hd:hd + 1], kbs[u] * egs[u]], axis=1))
    n_fac = int(math.log2(CHUNK))
    for it in range(n_fac):
        sols = [sol + _mm_hi(p, sol) for p, sol in zip(p_mats, sols)]
        if it + 1 < n_fac:
            p_mats = [_mm_hi(p, p) for p in p_mats]
    sts = [state_scr[u] for u in range(len(units))]
    ws = [_mm(sol[:, GDN_HEAD_DIM:], st) for sol, st in zip(sols, sts)]
    os_ = [_mm(qh * eg, st) for qh, eg, st in zip(qs, egs, sts)]
    v_news = [sol[:, :GDN_HEAD_DIM] - w for sol, w in zip(sols, ws)]
    os_ = [o + _mm(qk, vn) for o, qk, vn in zip(os_, qks, v_news)]
    for u, (b, hd) in enumerate(units):
        sl = slice(hd * GDN_HEAD_DIM, (hd + 1) * GDN_HEAD_DIM)
        glast = gcols[u][CHUNK - 1:CHUNK, :]
        kd = ks[u] * jnp.exp(glast - gcols[u])
        state_scr[u] = sts[u] * jnp.exp(glast) + _mm_tn(kd, v_news[u])
        o = os_[u]
        on = o * lax.rsqrt(jnp.mean(o * o, axis=-1, keepdims=True) + EPS) * nw
        ya_ref[b, :, sl] = (on * _silu(z_ref[b, :, sl])).astype(ya_ref.dtype)


def _gdn(q, k, v, z, gb, norm_w, batch, seq):
    nc = seq // CHUNK
    t_rows = q.shape[0]
    blk = lambda w: pl.BlockSpec((batch, CHUNK, w), lambda c: (0, c, 0))
    r3 = lambda a: a.reshape(batch, seq, a.shape[-1])
    out = pl.pallas_call(
        functools.partial(_gdn_kernel, batch=batch),
        grid=(nc,),
        in_specs=[blk(GDN_WIDTH)] * 4 + [blk(LANES), pl.BlockSpec((1, GDN_HEAD_DIM), lambda c: (0, 0))],
        out_specs=blk(GDN_WIDTH),
        out_shape=jax.ShapeDtypeStruct((batch, seq, GDN_WIDTH), BF16),
        scratch_shapes=[pltpu.VMEM((batch * GDN_HEADS, GDN_HEAD_DIM, GDN_HEAD_DIM), F32)],
        compiler_params=_cparams("arbitrary"),
        name="gdn",
    )(r3(q), r3(k), r3(v), r3(z), r3(gb), norm_w)
    return out.reshape(t_rows, GDN_WIDTH)


def _s5_tables(lam_re, lam_im, log_step, b_re, b_im, c_re, c_im, n_lev):
    hp = lax.Precision.HIGHEST
    lr = jnp.minimum(lam_re.astype(F32), -1e-4)
    li = lam_im.astype(F32)
    dt = jnp.exp(log_step.astype(F32))[:, None]
    js = jnp.arange(S5_Q + 1, dtype=F32)[:, None, None]
    mag = jnp.exp(lr * dt * js)
    pr, pi = mag * jnp.cos(li * dt * js), mag * jnp.sin(li * dt * js)
    ar, ai = pr[1], pi[1]
    nr, ni = ar - 1.0, ai
    den = lr * lr + li * li
    cr, ci = (nr * lr + ni * li) / den, (ni * lr - nr * li) / den
    b_re, b_im = b_re.astype(F32), b_im.astype(F32)
    bbr = cr[..., None] * b_re - ci[..., None] * b_im
    bbi = cr[..., None] * b_im + ci[..., None] * b_re
    c_re, c_im = c_re.astype(F32), c_im.astype(F32)
    lbr = pr[..., None] * bbr - pi[..., None] * bbi
    lbi = pr[..., None] * bbi + pi[..., None] * bbr
    kern = (jnp.einsum('gon,jgni->gjoi', c_re, lbr[:S5_Q], precision=hp)
            - jnp.einsum('gon,jgni->gjoi', c_im, lbi[:S5_Q], precision=hp))
    s_idx = jnp.arange(S5_Q)[:, None]
    t_idx = jnp.arange(S5_Q)[None, :]
    lag = jnp.clip(t_idx - s_idx, 0, S5_Q - 1)
    toep = jnp.where((t_idx >= s_idx)[None, :, :, None, None], kern[:, lag], 0.0)
    toep = toep.transpose(0, 1, 4, 2, 3).reshape(S5_GROUPS, S5_Q * S5_GROUP, S5_Q * S5_GROUP)
    er = lbr[:S5_Q][::-1].transpose(1, 0, 3, 2).reshape(S5_GROUPS, S5_Q * S5_GROUP, S5_STATE)
    ei = lbi[:S5_Q][::-1].transpose(1, 0, 3, 2).reshape(S5_GROUPS, S5_Q * S5_GROUP, S5_STATE)
    e_mat = jnp.concatenate([er, ei], axis=-1)
    clr = c_re[None] * pr[1:, :, None, :] - c_im[None] * pi[1:, :, None, :]
    cli = c_re[None] * pi[1:, :, None, :] + c_im[None] * pr[1:, :, None, :]
    fr = clr.transpose(1, 3, 0, 2).reshape(S5_GROUPS, S5_STATE, S5_Q * S5_GROUP)
    fi = (-cli).transpose(1, 3, 0, 2).reshape(S5_GROUPS, S5_STATE, S5_Q * S5_GROUP)
    f_mat = jnp.concatenate([fr, fi], axis=1)
    a1, a2 = [], []
    cur_r, cur_i = pr[S5_Q], pi[S5_Q]
    for _ in range(n_lev):
        a1.append(jnp.concatenate([cur_r, cur_r], axis=-1))
        a2.append(jnp.concatenate([-cur_i, cur_i], axis=-1))
        cur_r, cur_i = cur_r * cur_r - cur_i * cur_i, 2.0 * cur_r * cur_i
    pad = [jnp.zeros_like(a1[0])] * (8 - n_lev % 8 if n_lev % 8 else 0)
    a1 = jnp.stack(a1 + pad, axis=1)
    a2 = jnp.stack(a2 + pad, axis=1)
    return toep.astype(BF16), e_mat.astype(BF16), f_mat.astype(BF16), a1, a2


def _s5_conv_kernel(u_ref, t_ref, e_ref, f_ref, a1_ref, a2_ref, y_ref, *, seg, n_lev):
    u = u_ref[0]
    y1 = jnp.dot(u, t_ref[0], preferred_element_type=F32)
    x = jnp.dot(u, e_ref[0], preferred_element_type=F32)
    rin = lax.broadcasted_iota(jnp.int32, x.shape, 0) % seg
    for lev in range(n_lev):
        s = 1 << lev
        xs = jnp.where(rin >= s, pltpu.roll(x, s, 0), 0.0)
        x = x + a1_ref[0, lev:lev + 1, :] * xs + a2_ref[0, lev:lev + 1, :] * pltpu.roll(xs, S5_STATE, 1)
    xp = jnp.where(rin >= 1, pltpu.roll(x, 1, 0), 0.0)
    y_ref[0] = y1 + _mm(xp, f_ref[0])


def _s5_conv(u_grp, toep, e_mat, f_mat, a1, a2, seg, n_lev):
    g, m, w = u_grp.shape
    kern = functools.partial(_s5_conv_kernel, seg=seg, n_lev=n_lev)
    per_g = lambda a: pl.BlockSpec((1,) + a.shape[1:], lambda i: (i, 0, 0))
    return pl.pallas_call(
        kern,
        grid=(g,),
        in_specs=[per_g(u_grp), per_g(toep), per_g(e_mat), per_g(f_mat), per_g(a1), per_g(a2)],
        out_specs=pl.BlockSpec((1, m, w), lambda i: (i, 0, 0)),
        out_shape=jax.ShapeDtypeStruct((g, m, w), F32),
        compiler_params=_cparams("parallel"),
        name="s5_conv",
    )(u_grp, toep, e_mat, f_mat, a1, a2)


def _s5_glu_kernel(y_ref, u_ref, d_ref, w_ref, b_ref, o_ref):
    y = y_ref[...] + d_ref[...] * u_ref[...]
    y = 0.5 * y * (1.0 + lax.erf(y * (2.0 ** -0.5)))
    ab = _mm(y, w_ref[...]) + b_ref[...]
    o_ref[...] = (ab[:, :S5_WIDTH] * jax.nn.sigmoid(ab[:, S5_WIDTH:])).astype(o_ref.dtype)


def _s5_glu(yconv, u, d_skip, glu_w, glu_b):
    t_rows = u.shape[0]
    tm = _row_tile(t_rows, 512)
    row = pl.BlockSpec((tm, S5_WIDTH), lambda i: (i, 0))
    full = lambda a: pl.BlockSpec(a.shape, lambda i: (0,) * a.ndim)
    return pl.pallas_call(
        _s5_glu_kernel,
        grid=(t_rows // tm,),
        in_specs=[row, row, full(d_skip), full(glu_w), full(glu_b)],
        out_specs=row,
        out_shape=jax.ShapeDtypeStruct((t_rows, S5_WIDTH), BF16),
        compiler_params=_cparams("parallel"),
        name="s5_glu",
    )(yconv, u, d_skip, glu_w, glu_b)


def _proj_res_kernel(*refs, n_in):
    x_ref, g_ref = refs[0], refs[1]
    ys = refs[2:2 + n_in]
    ws = refs[2 + n_in:2 + 2 * n_in]
    o_ref = refs[2 + 2 * n_in]
    acc = jnp.dot(ys[0][...], ws[0][...], preferred_element_type=F32)
    for y, w in zip(ys[1:], ws[1:]):
        acc = acc + jnp.dot(y[...], w[...], preferred_element_type=F32)
    o_ref[...] = x_ref[...] + g_ref[0] * acc


def _proj_res(x, gate, seq, ys, ws):
    t_rows = x.shape[0]
    tm = _row_tile(seq, 512)
    n_in = len(ys)
    full = lambda a: pl.BlockSpec(a.shape, lambda i: (0,) * a.ndim)
    return pl.pallas_call(
        functools.partial(_proj_res_kernel, n_in=n_in),
        grid=(t_rows // tm,),
        in_specs=([pl.BlockSpec((tm, D_MODEL), lambda i: (i, 0)), _mod_spec(tm, seq)]
                  + [pl.BlockSpec((tm, y.shape[1]), lambda i: (i, 0)) for y in ys]
                  + [full(w) for w in ws]),
        out_specs=pl.BlockSpec((tm, D_MODEL), lambda i: (i, 0)),
        out_shape=jax.ShapeDtypeStruct(x.shape, F32),
        compiler_params=_cparams("parallel"),
        name="proj_res",
    )(x, gate, *ys, *ws)


def _proj_res_t_kernel(x_ref, g_ref, yt_ref, w_ref, o_ref):
    acc = lax.dot_general(yt_ref[...], w_ref[...], (((0,), (0,)), ((), ())), preferred_element_type=F32)
    o_ref[...] = x_ref[...] + g_ref[0] * acc


def _proj_res_t(x, gate, seq, yt, w):
    t_rows = x.shape[0]
    tm = _row_tile(seq, 512)
    row = pl.BlockSpec((tm, D_MODEL), lambda i: (i, 0))
    return pl.pallas_call(
        _proj_res_t_kernel,
        grid=(t_rows // tm,),
        in_specs=[row, _mod_spec(tm, seq), pl.BlockSpec((yt.shape[0], tm), lambda i: (0, i)),
                  pl.BlockSpec(w.shape, lambda i: (0, 0))],
        out_specs=row,
        out_shape=jax.ShapeDtypeStruct(x.shape, F32),
        compiler_params=_cparams("parallel"),
        name="proj_res_t",
    )(x, gate, yt, w)


def _ffn_kernel(x_ref, nw_ref, sc_ref, sh_ref, g_ref, w1_ref, w3_ref, w2_ref, o_ref):
    x = x_ref[...]
    h = _norm_mod(x, nw_ref[...], sc_ref[0], sh_ref[0]).astype(BF16)
    a = jnp.dot(h, w1_ref[...], preferred_element_type=F32)
    b = jnp.dot(h, w3_ref[...], preferred_element_type=F32)
    f = jnp.dot((_silu(a) * b).astype(BF16), w2_ref[...], preferred_element_type=F32)
    o_ref[...] = x + g_ref[0] * f


def _ffn(x, seq, nw, sc, sh, gate, w13, w2):
    t_rows = x.shape[0]
    tm = _row_tile(seq, 256)
    row = pl.BlockSpec((tm, D_MODEL), lambda i: (i, 0))
    once = lambda a: pl.BlockSpec(a.shape, lambda i: (0,) * a.ndim, pipeline_mode=pl.Buffered(1))
    half = lambda j: pl.BlockSpec((D_MODEL, D_FF), lambda i: (0, j), pipeline_mode=pl.Buffered(1))
    return pl.pallas_call(
        _ffn_kernel,
        grid=(t_rows // tm,),
        in_specs=[row, once(nw), _mod_spec(tm, seq), _mod_spec(tm, seq), _mod_spec(tm, seq),
                  half(0), half(1), once(w2)],
        out_specs=row,
        out_shape=jax.ShapeDtypeStruct(x.shape, F32),
        compiler_params=_cparams("parallel"),
        name="ffn",
    )(x, nw, sc, sh, gate, w13, w13, w2)


LOG2E = 1.4426950408889634


def _odd_qkv_kernel(x_ref, nw_ref, sc_ref, sh_ref, w_ref, wvt_ref, qw_ref, kw_ref, q_ref, k_ref, vt_ref):
    h = _norm_mod(x_ref[...], nw_ref[...], sc_ref[0], sh_ref[0]).astype(BF16)
    proj = jnp.dot(h, w_ref[...], preferred_element_type=F32)
    r = lax.broadcasted_iota(jnp.int32, (LANES, LANES), 0) // DIFF_HEAD_DIM
    c = lax.broadcasted_iota(jnp.int32, (LANES, LANES), 1) // DIFF_HEAD_DIM
    grp = (r == c).astype(BF16)
    q_scale = DIFF_HEAD_DIM ** -0.5 * LOG2E
    for base, w_vec, scale, out in ((0, qw_ref, q_scale, q_ref), (D_MODEL, kw_ref, 1.0, k_ref)):
        for t in range(D_MODEL // LANES):
            xt = proj[:, base + t * LANES:base + (t + 1) * LANES]
            ss = sum(jnp.dot(p, grp, preferred_element_type=F32) for p in _split3(xt * xt))
            y = xt * lax.rsqrt(ss * (1.0 / DIFF_HEAD_DIM) + EPS) * w_vec[...]
            out[:, t * LANES:(t + 1) * LANES] = (y * scale).astype(out.dtype)
    vt_ref[...] = _mm_nt(wvt_ref[...], h).astype(vt_ref.dtype)


def _odd_qkv(x, seq, nw, sc, sh, w_qk, w_vt, qw, kw):
    t_rows = x.shape[0]
    tm = _row_tile(seq, 512)
    row = pl.BlockSpec((tm, D_MODEL), lambda i: (i, 0))
    full = lambda a: pl.BlockSpec(a.shape, lambda i: (0,) * a.ndim)
    out = jax.ShapeDtypeStruct((t_rows, D_MODEL), BF16)
    return pl.pallas_call(
        _odd_qkv_kernel,
        grid=(t_rows // tm,),
        in_specs=[row, full(nw), _mod_spec(tm, seq), _mod_spec(tm, seq), full(w_qk), full(w_vt), full(qw),
                  full(kw)],
        out_specs=[row, row, pl.BlockSpec((D_MODEL, tm), lambda i: (0, i))],
        out_shape=[out, out, jax.ShapeDtypeStruct((D_MODEL, t_rows), BF16)],
        compiler_params=_cparams("parallel"),
        name="odd_qkv",
    )(x, nw, sc, sh, w_qk, w_vt, qw, kw)


def _attn_kernel(bound_ref, q_ref, k_ref, vt_ref, lam_ref, sw_ref, o_ref, acc_scr, pp_scr, l_scr,
                 *, tq, tk, heads, out_scale):
    i = pl.program_id(2)
    lane = lax.broadcasted_iota(jnp.int32, (tq, LANES), 1)
    qs = []
    for hh in range(heads):
        q = q_ref[:, hh * LANES:(hh + 1) * LANES]
        zero = jnp.zeros_like(q)
        qs += [jnp.where(lane < DIFF_HEAD_DIM, q, zero), jnp.where(lane >= DIFF_HEAD_DIM, q, zero)]
    n_sub = tq // LANES
    n_full = i * (tq // tk)

    def load_k(j, hh):
        k0 = pl.multiple_of(j * tk, tk)
        return k0, k_ref[pl.ds(k0, tk), hh * LANES:(hh + 1) * LANES]

    def load_v(j, hh):
        return vt_ref[hh * LANES:(hh + 1) * LANES, pl.ds(pl.multiple_of(j * tk, tk), tk)]

    def scores(kb, u, c, k0, masked):
        s = lax.dot_general(kb, qs[u][c * LANES:(c + 1) * LANES, :], (((1,), (1,)), ((), ())),
                            preferred_element_type=F32)
        if masked:
            kc = (k0 + lax.broadcasted_iota(jnp.int32, (tk, LANES), 0)) // CHUNK
            qc = (i * tq + c * LANES + lax.broadcasted_iota(jnp.int32, (tk, LANES), 1)) // CHUNK
            s = jnp.where(kc <= qc, s, -1e30)
        return s

    def finish(ls):
        lam = lam_ref[0:1, 0:1]
        for hh in range(heads):
            o = acc_scr[2 * hh] / ls[2 * hh] - lam * (acc_scr[2 * hh + 1] / ls[2 * hh + 1])
            o = o * lax.rsqrt(jnp.mean(o * o, axis=0, keepdims=True) + EPS) * sw_ref[...] * out_scale
            o_ref[hh * LANES:(hh + 1) * LANES, :] = o.astype(o_ref.dtype)

    acc_scr[...] = jnp.zeros_like(acc_scr)

    @pl.when(bound_ref[0] == 1)
    def _():
        l_scr[...] = jnp.zeros_like(l_scr)
        pp_scr[1] = jnp.zeros(pp_scr.shape[1:], BF16)

        def values(j, hh):
            vb = load_v(j, hh)
            for u in (2 * hh, 2 * hh + 1):
                acc_scr[u] = acc_scr[u] + jnp.dot(vb, pp_scr[j % 2, u], preferred_element_type=F32)

        def step(j, masked):
            jp = jnp.maximum(j - 1, 0) + (j == 0).astype(jnp.int32)
            for hh in range(heads):
                values(jp, hh)
                k0, kb = load_k(j, hh)
                for u in (2 * hh, 2 * hh + 1):
                    for c in range(n_sub):
                        p = jnp.exp2(scores(kb, u, c, k0, masked))
                        cs = slice(c * LANES, (c + 1) * LANES)
                        l_scr[u, :, cs] = l_scr[u, :, cs] + jnp.sum(p.reshape(tk // 8, 8, LANES), axis=0)
                        pp_scr[j % 2, u, :, cs] = p.astype(BF16)

        def body(j, carry):
            step(j, False)
            return carry

        lax.fori_loop(0, n_full, body, 0)
        n_mask = tq // tk
        for jj in range(n_mask):
            step(n_full + jj, True)
        for hh in range(heads):
            values(n_full + n_mask - 1, hh)
        finish([jnp.sum(l_scr[u], axis=0, keepdims=True) for u in range(2 * heads)])

    @pl.when(bound_ref[0] != 1)
    def _():
        neg = jnp.full((1, tq), -1e30, F32)
        zer = jnp.zeros((1, tq), F32)

        def block(j, carry, masked):
            out = []
            for hh in range(heads):
                k0, kb = load_k(j, hh)
                vb = load_v(j, hh)
                alphas = []
                for u in (2 * hh, 2 * hh + 1):
                    m, l = carry[2 * u], carry[2 * u + 1]
                    m_parts, l_parts, a_parts = [], [], []
                    for c in range(n_sub):
                        cs = slice(c * LANES, (c + 1) * LANES)
                        s = scores(kb, u, c, k0, masked)
                        m_new = jnp.maximum(m[:, cs], jnp.max(s, axis=0, keepdims=True))
                        p = jnp.exp2(s - m_new)
                        alpha = jnp.exp2(m[:, cs] - m_new)
                        l_parts.append(alpha * l[:, cs] + jnp.sum(p, axis=0, keepdims=True))
                        pp_scr[0, u, :, cs] = p.astype(BF16)
                        m_parts.append(m_new)
                        a_parts.append(alpha)
                    out += [jnp.concatenate(m_parts, axis=1), jnp.concatenate(l_parts, axis=1)]
                    alphas.append(jnp.concatenate(a_parts, axis=1))
                for t, u in enumerate((2 * hh, 2 * hh + 1)):
                    acc_scr[u] = alphas[t] * acc_scr[u] + jnp.dot(vb, pp_scr[0, u], preferred_element_type=F32)
            return tuple(out)

        carry = lax.fori_loop(0, n_full, lambda j, c: block(j, c, False), (neg, zer) * (2 * heads))
        for jj in range(tq // tk):
            carry = block(n_full + jj, carry, True)
        finish([carry[2 * u + 1] for u in range(2 * heads)])


ATTN_HEADS_PER_STEP = 4
SCORE_BOUND_LOG2 = 64.0


def _attention(q, k, vt, score_bound, lam_vec, subln_col, batch, seq, out_scale):
    tq = _row_tile(seq, 512)
    tk = _row_tile(tq, 256)
    nq = seq // tq
    assert seq // tk >= 2
    hp = ATTN_HEADS_PER_STEP
    wid = hp * LANES
    kern = functools.partial(_attn_kernel, tq=tq, tk=tk, heads=hp, out_scale=out_scale)
    full = lambda a: pl.BlockSpec(a.shape, lambda b, h, i: (0,) * a.ndim)
    bounded = (score_bound <= SCORE_BOUND_LOG2).astype(jnp.int32).reshape(1)
    return pl.pallas_call(
        kern,
        grid=(batch, DIFF_HEADS // hp, nq),
        in_specs=[pl.BlockSpec(memory_space=pltpu.SMEM),
                  pl.BlockSpec((tq, wid), lambda b, h, i: (b * nq + i, h)),
                  pl.BlockSpec((seq, wid), lambda b, h, i: (b, h)),
                  pl.BlockSpec((wid, seq), lambda b, h, i: (h, b)),
                  full(lam_vec), full(subln_col)],
        out_specs=pl.BlockSpec((wid, tq), lambda b, h, i: (h, b * nq + i)),
        out_shape=jax.ShapeDtypeStruct(vt.shape, BF16),
        scratch_shapes=[pltpu.VMEM((2 * hp, LANES, tq), F32),
                        pltpu.VMEM((2, 2 * hp, tk, tq), BF16), pltpu.VMEM((2 * hp, 8, tq), F32)],
        compiler_params=_cparams("parallel", "parallel", "arbitrary"),
        name="diff_attn",
    )(bounded, q, k, vt, lam_vec, subln_col)


def _router_kernel(x_ref, nw_ref, sc_ref, sh_ref, whi_ref, wlo_ref, h_ref, idx_ref, gate_ref):
    h = _norm_mod(x_ref[...], nw_ref[...], sc_ref[0], sh_ref[0])
    h_ref[...] = h
    h_hi = h.astype(BF16)
    h_lo = (h - h_hi.astype(F32)).astype(BF16)
    logits = (jnp.dot(h_hi, whi_ref[...], preferred_element_type=F32)
              + jnp.dot(h_hi, wlo_ref[...], preferred_element_type=F32)
              + jnp.dot(h_lo, whi_ref[...], preferred_element_type=F32))
    lane = lax.broadcasted_iota(jnp.int32, logits.shape, 1)
    lane_f = lane.astype(F32)
    logits = jnp.where(lane < N_EXPERTS, logits, -jnp.inf)
    m1 = jnp.max(logits, axis=-1, keepdims=True)
    i1 = jnp.min(jnp.where(logits == m1, lane_f, float(LANES)), axis=-1, keepdims=True)
    rest = jnp.where(lane_f == i1, -jnp.inf, logits)
    m2 = jnp.max(rest, axis=-1, keepdims=True)
    i2 = jnp.min(jnp.where(rest == m2, lane_f, float(LANES)), axis=-1, keepdims=True)
    e = jnp.exp(m2 - m1)
    g1 = 1.0 / (1.0 + e)
    idx_ref[...] = jnp.where(lane == 0, i1, jnp.where(lane == 1, i2, 0.0)).astype(jnp.int32)
    gate_ref[...] = jnp.where(lane == 0, g1, jnp.where(lane == 1, e * g1, 0.0))


def _router(x, seq, nw, sc, sh, w_hi, w_lo):
    t_rows = x.shape[0]
    tm = _row_tile(seq, 512)
    row = lambda w: pl.BlockSpec((tm, w), lambda i: (i, 0))
    full = lambda a: pl.BlockSpec(a.shape, lambda i: (0,) * a.ndim)
    return pl.pallas_call(
        _router_kernel,
        grid=(t_rows // tm,),
        in_specs=[row(D_MODEL), full(nw), _mod_spec(tm, seq), _mod_spec(tm, seq), full(w_hi), full(w_lo)],
        out_specs=[row(D_MODEL), row(LANES), row(LANES)],
        out_shape=[jax.ShapeDtypeStruct((t_rows, D_MODEL), F32),
                   jax.ShapeDtypeStruct((t_rows, LANES), jnp.int32),
                   jax.ShapeDtypeStruct((t_rows, LANES), F32)],
        compiler_params=_cparams("parallel"),
        name="router",
    )(x, nw, sc, sh, w_hi, w_lo)


GATHER_ROWS = 256


def _row_copy(src_hbm, dst_ref, sem, src_row, dst_row):
    return pltpu.make_async_copy(src_hbm.at[pl.ds(src_row, 1), :], dst_ref.at[pl.ds(dst_row, 1), :], sem)


def _gather_kernel(idx_ref, src_hbm, o_ref, sem):
    def start(r2, _):
        for prio in range(2):
            r = 2 * r2 + prio
            _row_copy(src_hbm, o_ref, sem, idx_ref[0, 0, r], r).start(priority=prio)
        return 0

    lax.fori_loop(0, GATHER_ROWS // 2, start, 0, unroll=4)
    pltpu.make_async_copy(src_hbm.at[pl.ds(0, GATHER_ROWS), :], o_ref, sem).wait()


def _gather_rows(src, idx):
    n = idx.shape[0]
    width = src.shape[1]
    return pl.pallas_call(
        _gather_kernel,
        grid=(n // GATHER_ROWS,),
        in_specs=[pl.BlockSpec((1, 1, GATHER_ROWS), lambda i: (i, 0, 0), memory_space=pltpu.SMEM),
                  pl.BlockSpec(memory_space=pl.ANY)],
        out_specs=pl.BlockSpec((GATHER_ROWS, width), lambda i: (i, 0)),
        scratch_shapes=[pltpu.SemaphoreType.DMA(())],
        out_shape=jax.ShapeDtypeStruct((n, width), src.dtype),
        compiler_params=_cparams("arbitrary"),
        name="gather_rows",
    )(idx.reshape(n // GATHER_ROWS, 1, GATHER_ROWS), src)


def _expert_kernel(te_ref, nt_ref, x_ref, w1_ref, w3_ref, w2_ref, o_ref):
    i = pl.program_id(0)

    @pl.when(i < nt_ref[0])
    def _():
        h = x_ref[...].astype(BF16)
        a = jnp.dot(h, w1_ref[0], preferred_element_type=F32)
        b = jnp.dot(h, w3_ref[0], preferred_element_type=F32)
        o_ref[...] = jnp.dot((_silu(a) * b).astype(BF16), w2_ref[0], preferred_element_type=F32)

    @pl.when(i >= nt_ref[0])
    def _():
        o_ref[...] = jnp.zeros_like(o_ref)


EXPERT_TM = 256


def _expert_ffn(x_perm, tile_expert, n_tiles_used, w13, w2):
    p_rows = x_perm.shape[0]
    tm = EXPERT_TM
    return pl.pallas_call(
        _expert_kernel,
        grid_spec=pltpu.PrefetchScalarGridSpec(
            num_scalar_prefetch=2,
            grid=(p_rows // tm,),
            in_specs=[pl.BlockSpec((tm, D_MODEL), lambda i, te, nt: (i, 0)),
                      pl.BlockSpec((1, D_MODEL, D_FF), lambda i, te, nt: (te[i], 0, 0)),
                      pl.BlockSpec((1, D_MODEL, D_FF), lambda i, te, nt: (te[i], 0, 1)),
                      pl.BlockSpec((1, D_FF, D_MODEL), lambda i, te, nt: (te[i], 0, 0))],
            out_specs=pl.BlockSpec((tm, D_MODEL), lambda i, te, nt: (i, 0)),
        ),
        out_shape=jax.ShapeDtypeStruct((p_rows, D_MODEL), F32),
        compiler_params=_cparams("arbitrary"),
        name="expert_ffn",
    )(tile_expert, n_tiles_used, x_perm, w13, w13, w2)


def _combine_kernel(x_ref, g_ref, gate_ref, y0_ref, y1_ref, o_ref):
    gates = gate_ref[...]
    f = gates[:, 0:1] * y0_ref[...] + gates[:, 1:2] * y1_ref[...]
    o_ref[...] = x_ref[...] + g_ref[0] * f


def _combine(x, gate_mod, seq, gates, y_tok):
    t_rows = x.shape[0]
    tm = _row_tile(seq, 512)
    nt = t_rows // tm
    row = pl.BlockSpec((tm, D_MODEL), lambda i: (i, 0))
    return pl.pallas_call(
        _combine_kernel,
        grid=(nt,),
        in_specs=[row, _mod_spec(tm, seq), pl.BlockSpec((tm, LANES), lambda i: (i, 0)),
                  row, pl.BlockSpec((tm, D_MODEL), lambda i: (i + nt, 0))],
        out_specs=row,
        out_shape=jax.ShapeDtypeStruct(x.shape, F32),
        compiler_params=_cparams("parallel"),
        name="moe_combine",
    )(x, gate_mod, gates, y_tok, y_tok)


def _dispatch_plan(expert_idx, t_rows):
    tm = EXPERT_TM
    flat = expert_idx.reshape(-1)
    onehot = (flat[:, None] == jnp.arange(N_EXPERTS)[None, :]).astype(jnp.int32)
    rank = jnp.take_along_axis(jnp.cumsum(onehot, axis=0) - onehot, flat[:, None], axis=1)[:, 0]
    counts = jnp.sum(onehot, axis=0)
    tiles_per = (counts + tm - 1) // tm
    tile_end = jnp.cumsum(tiles_per)
    offs = (tile_end - tiles_per) * tm
    pos = offs[flat] + rank
    n_tiles = (2 * t_rows) // tm + N_EXPERTS
    p_rows = n_tiles * tm
    tok_of_slot = jnp.zeros((p_rows,), jnp.int32).at[pos].set(jnp.arange(2 * t_rows, dtype=jnp.int32) // 2)
    tile_expert = jnp.minimum(
        jnp.sum(jnp.arange(n_tiles, dtype=jnp.int32)[:, None] >= tile_end[None, :], axis=1), N_EXPERTS - 1
    ).astype(jnp.int32)
    n_used = tile_end[-1:].astype(jnp.int32)
    pos2 = pos.reshape(t_rows, 2)
    gather_back = jnp.concatenate([pos2[:, 0], pos2[:, 1]]).astype(jnp.int32)
    return tok_of_slot, tile_expert, n_used, gather_back


def _mods(mod, layer, batch):
    m = mod[layer, :batch].reshape(batch, 6, 1, D_MODEL)
    return [m[:, j] for j in range(6)]


def _even_layer(x, batch, seq, mods, nw_mix, nw_ffn, w_in, conv_w, a_log, dt_bias, gdn_norm_w,
                lam_re, lam_im, log_step, b_re, b_im, c_re, c_im, d_skip, glu_w, glu_b, w_out, w13, w2):
    sh1, sc1, g1, sh2, sc2, g2 = mods
    t_rows = batch * seq
    qkvz, rest = w_in[:, :4 * GDN_WIDTH], w_in[:, 4 * GDN_WIDTH:]
    w_ba, w_u = rest[:, :2 * GDN_HEADS], rest[:, 2 * GDN_HEADS:]
    w_cat = jnp.concatenate(
        [qkvz, w_u, w_ba, jnp.zeros((D_MODEL, LANES - 2 * GDN_HEADS), w_in.dtype)], axis=1).astype(BF16)
    gate_params = jnp.zeros((8, LANES), F32)
    gate_params = gate_params.at[0, GDN_HEADS:2 * GDN_HEADS].set(a_log.astype(F32))
    gate_params = gate_params.at[1, GDN_HEADS:2 * GDN_HEADS].set(dt_bias.astype(F32))
    q, k, v, z, u, gb = _even_in(x, seq, nw_mix, sc1, sh1, w_cat, conv_w.astype(F32), gate_params)
    y_a = _gdn(q, k, v, z, gb, gdn_norm_w.reshape(1, GDN_HEAD_DIM).astype(F32), batch, seq)

    seg = seq // S5_Q
    n_lev = max(int(math.ceil(math.log2(seg))), 0)
    toep, e_mat, f_mat, a1, a2 = _s5_tables(lam_re, lam_im, log_step, b_re, b_im, c_re, c_im, max(n_lev, 1))
    m_rows = t_rows // S5_Q
    u_grp = (u.reshape(m_rows, S5_Q, S5_GROUPS, S5_GROUP).transpose(2, 0, 1, 3)
             .reshape(S5_GROUPS, m_rows, S5_Q * S5_GROUP).astype(BF16))
    yg = _s5_conv(u_grp, toep, e_mat, f_mat, a1, a2, seg, n_lev)
    yconv = (yg.reshape(S5_GROUPS, m_rows, S5_Q, S5_GROUP).transpose(1, 2, 0, 3)
             .reshape(t_rows, S5_WIDTH))
    y_b = _s5_glu(yconv, u, d_skip.reshape(1, S5_WIDTH).astype(F32), glu_w.astype(BF16),
                  glu_b.reshape(1, 2 * S5_WIDTH).astype(F32))
    wo = w_out.astype(BF16)
    x = _proj_res(x, g1, seq, [y_a, y_b], [wo[:GDN_WIDTH], wo[GDN_WIDTH:]])
    return _ffn(x, seq, nw_ffn, sc2, sh2, g2, w13.astype(BF16), w2.astype(BF16))


def _odd_layer(x, batch, seq, mods, nw_mix, nw_ffn, w_qkv, q_norm_w, k_norm_w, lq1, lk1, lq2, lk2,
               subln_w, w_out, router_w, w13, w2, lambda_init):
    sh1, sc1, g1, sh2, sc2, g2 = mods
    t_rows = batch * seq
    tile2 = lambda w: jnp.tile(w.astype(F32), LANES // DIFF_HEAD_DIM).reshape(1, LANES)
    w_qkv_b = w_qkv.astype(BF16)
    q, k, vt = _odd_qkv(x, seq, nw_mix, sc1, sh1, w_qkv_b[:, :2 * D_MODEL], w_qkv_b[:, 2 * D_MODEL:].T,
                        tile2(q_norm_w), tile2(k_norm_w))
    lam = (jnp.exp(jnp.sum(lq1.astype(F32) * lk1.astype(F32)))
           - jnp.exp(jnp.sum(lq2.astype(F32) * lk2.astype(F32))) + lambda_init)
    lam_vec = jnp.full((8, LANES), lam, F32)
    score_bound = (1.02 * LOG2E * DIFF_HEAD_DIM ** 0.5
                   * jnp.max(jnp.abs(q_norm_w.astype(F32))) * jnp.max(jnp.abs(k_norm_w.astype(F32))))
    ot = _attention(q, k, vt, score_bound, lam_vec, subln_w.reshape(LANES, 1).astype(F32), batch, seq,
                    1.0 - lambda_init)
    x = _proj_res_t(x, g1, seq, ot, w_out.astype(BF16))

    rw = jnp.zeros((D_MODEL, LANES), F32).at[:, :N_EXPERTS].set(router_w.astype(F32))
    rw_hi = rw.astype(BF16)
    rw_lo = (rw - rw_hi.astype(F32)).astype(BF16)
    h, idx, gates = _router(x, seq, nw_ffn, sc2, sh2, rw_hi, rw_lo)
    tok_of_slot, tile_expert, n_used, gather_back = _dispatch_plan(idx[:, :2], t_rows)
    h_perm = _gather_rows(h, tok_of_slot)
    y_perm = _expert_ffn(h_perm, tile_expert, n_used, w13.astype(BF16), w2.astype(BF16))
    y_tok = _gather_rows(y_perm, gather_back)
    return _combine(x, g2, seq, gates, y_tok)


def kernel(x, c, ada_w, ada_b, norm_mix_w, norm_ffn_w, even_w_in, even_conv_w, even_a_log, even_dt_bias, even_gdn_norm_w, even_lam_re, even_lam_im, even_log_step, even_b_re, even_b_im, even_c_re, even_c_im, even_d_skip, even_glu_w, even_glu_b, even_w_out, even_ffn_w13, even_ffn_w2, odd_w_qkv, odd_q_norm_w, odd_k_norm_w, odd_lambda_q1, odd_lambda_k1, odd_lambda_q2, odd_lambda_k2, odd_subln_w, odd_w_out, odd_router_w, odd_expert_w13, odd_expert_w2):
    batch, seq, d = x.shape
    assert d == D_MODEL and seq % CHUNK == 0 and seq % S5_Q == 0
    c_pad = jnp.zeros((8, d), F32).at[:batch].set(c.astype(F32))
    mod = _adaln(c_pad, ada_w.astype(BF16), ada_b.astype(F32))
    xf = x.astype(F32).reshape(batch * seq, d)
    for layer in range(DEPTH):
        i = layer // 2
        mods = _mods(mod, layer, batch)
        nw_mix = norm_mix_w[layer].reshape(1, d).astype(F32)
        nw_ffn = norm_ffn_w[layer].reshape(1, d).astype(F32)
        if layer % 2 == 0:
            xf = _even_layer(xf, batch, seq, mods, nw_mix, nw_ffn, even_w_in[i], even_conv_w[i], even_a_log[i],
                             even_dt_bias[i], even_gdn_norm_w[i], even_lam_re[i], even_lam_im[i],
                             even_log_step[i], even_b_re[i], even_b_im[i], even_c_re[i], even_c_im[i],
                             even_d_skip[i], even_glu_w[i], even_glu_b[i], even_w_out[i],
                             even_ffn_w13[i], even_ffn_w2[i])
        else:
            lambda_init = 0.8 - 0.6 * math.exp(-0.3 * layer)
            xf = _odd_layer(xf, batch, seq, mods, nw_mix, nw_ffn, odd_w_qkv[i], odd_q_norm_w[i], odd_k_norm_w[i],
                            odd_lambda_q1[i], odd_lambda_k1[i], odd_lambda_q2[i], odd_lambda_k2[i],
                            odd_subln_w[i], odd_w_out[i], odd_router_w[i], odd_expert_w13[i],
                            odd_expert_w2[i], lambda_init)
    return xf.reshape(batch, seq, d).astype(x.dtype)
```

```python
import functools
import math

import jax
import jax.numpy as jnp
from jax import lax
from jax.experimental import pallas as pl
from jax.experimental.pallas import tpu as pltpu

F32 = jnp.float32
BF16 = jnp.bfloat16

D_MODEL = 1024
DEPTH = 4
CHUNK = 64
EPS = 1e-6
GDN_HEAD_DIM = 128
GDN_WIDTH = D_MODEL // 2
GDN_HEADS = GDN_WIDTH // GDN_HEAD_DIM
CONV_K = 4
QKV_W = 3 * GDN_WIDTH
S5_WIDTH = D_MODEL - GDN_WIDTH
S5_GROUP = 16
S5_GROUPS = S5_WIDTH // S5_GROUP
S5_STATE = 64
S5_Q = 16
DIFF_HEAD_DIM = 64
DIFF_HEADS = D_MODEL // (2 * DIFF_HEAD_DIM)
D_FF = ((8 * D_MODEL // 3 + 127) // 128) * 128
N_EXPERTS = 8
LANES = 128
EVEN_IN_COLS = QKV_W + GDN_WIDTH + S5_WIDTH + LANES
VMEM_LIMIT = 56 * 1024 * 1024


def _cparams(*sem):
    return pltpu.CompilerParams(dimension_semantics=sem, vmem_limit_bytes=VMEM_LIMIT)


def _mm(a, b):
    return jnp.dot(a.astype(BF16), b.astype(BF16), preferred_element_type=F32)


def _mm_nt(a, b):
    return lax.dot_general(a.astype(BF16), b.astype(BF16), (((1,), (1,)), ((), ())),
                           preferred_element_type=F32)


def _mm_tn(a, b):
    return lax.dot_general(a.astype(BF16), b.astype(BF16), (((0,), (0,)), ((), ())),
                           preferred_element_type=F32)


def _mm_hi(a, b):
    a_hi = a.astype(BF16)
    a_lo = (a - a_hi.astype(F32)).astype(BF16)
    b_hi = b.astype(BF16)
    b_lo = (b - b_hi.astype(F32)).astype(BF16)
    return (jnp.dot(a_hi, b_hi, preferred_element_type=F32)
            + (jnp.dot(a_hi, b_lo, preferred_element_type=F32) + jnp.dot(a_lo, b_hi, preferred_element_type=F32)))


def _split3(x):
    hi = x.astype(BF16)
    r = x - hi.astype(F32)
    mid = r.astype(BF16)
    lo = (r - mid.astype(F32)).astype(BF16)
    return hi, mid, lo


def _norm_mod(x, nw, sc, sh):
    ms = jnp.mean(x * x, axis=-1, keepdims=True)
    return (x * lax.rsqrt(ms + EPS) * nw) * (1.0 + sc) + sh


def _silu(x):
    return x * jax.nn.sigmoid(x)


def _row_tile(n, pref):
    t = min(pref, n)
    while n % t:
        t //= 2
    return t


def _adaln_kernel(c_ref, w_ref, b_ref, o_ref):
    c = c_ref[...]
    o_ref[0] = _mm(_silu(c), w_ref[0]) + b_ref[0]


def _adaln(c_pad, ada_w, ada_b):
    depth, d, n = ada_w.shape
    tn = 1536
    rows = c_pad.shape[0]
    return pl.pallas_call(
        _adaln_kernel,
        grid=(depth, n // tn),
        in_specs=[pl.BlockSpec((rows, d), lambda l, j: (0, 0)),
                  pl.BlockSpec((1, d, tn), lambda l, j: (l, 0, j)),
                  pl.BlockSpec((1, 1, tn), lambda l, j: (l, 0, j))],
        out_specs=pl.BlockSpec((1, rows, tn), lambda l, j: (l, 0, j)),
        out_shape=jax.ShapeDtypeStruct((depth, rows, n), F32),
        compiler_params=_cparams("parallel", "parallel"),
        name="adaln",
    )(c_pad, ada_w, ada_b.reshape(depth, 1, n))


def _mod_spec(tm, seq):
    return pl.BlockSpec((1, 1, D_MODEL), lambda i, *_: ((i * tm) // seq, 0, 0))


def _even_in_kernel(x_ref, xh_ref, nw_ref, sc_ref, sh_ref, w_ref, cw_ref, gp_ref,
                    q_ref, k_ref, v_ref, z_ref, u_ref, gb_ref, pre_scr, *, tm, seq):
    i = pl.program_id(0)
    nw = nw_ref[...]
    sc = sc_ref[0]
    sh = sh_ref[0]
    h = _norm_mod(x_ref[...], nw, sc, sh).astype(BF16)
    proj = jnp.dot(h, w_ref[...], preferred_element_type=F32)
    hh = _norm_mod(xh_ref[...], nw, sc, sh).astype(BF16)
    preh = jnp.dot(hh, w_ref[:, :QKV_W], preferred_element_type=F32)
    preh = jnp.where((i * tm) % seq == 0, 0.0, preh)
    pre = proj[:, :QKV_W]
    pre_scr[0:8, :] = preh
    pre_scr[8:8 + tm, :] = pre
    acc = pre * cw_ref[CONV_K - 1:CONV_K, :]
    for j in range(CONV_K - 1):
        acc = acc + pre_scr[pl.ds(8 - (CONV_K - 1) + j, tm), :] * cw_ref[j:j + 1, :]
    qkv = _silu(acc)
    for hd in range(GDN_HEADS):
        lo = hd * GDN_HEAD_DIM
        qh = qkv[:, lo:lo + GDN_HEAD_DIM]
        kh = qkv[:, GDN_WIDTH + lo:GDN_WIDTH + lo + GDN_HEAD_DIM]
        q_ref[:, lo:lo + GDN_HEAD_DIM] = (
            qh * lax.rsqrt(jnp.sum(qh * qh, axis=-1, keepdims=True) + EPS) * GDN_HEAD_DIM ** -0.5)
        k_ref[:, lo:lo + GDN_HEAD_DIM] = kh * lax.rsqrt(jnp.sum(kh * kh, axis=-1, keepdims=True) + EPS)
    v_ref[...] = qkv[:, 2 * GDN_WIDTH:]
    z_ref[...] = proj[:, QKV_W:QKV_W + GDN_WIDTH]
    u_ref[...] = proj[:, QKV_W + GDN_WIDTH:QKV_W + GDN_WIDTH + S5_WIDTH]
    ba = proj[:, QKV_W + GDN_WIDTH + S5_WIDTH:]
    lane = lax.broadcasted_iota(jnp.int32, ba.shape, 1)
    t = ba + gp_ref[1:2, :]
    softplus = jnp.maximum(t, 0.0) + jnp.log1p(jnp.exp(-jnp.abs(t)))
    g = -jnp.exp(gp_ref[0:1, :]) * softplus
    gb_ref[...] = jnp.where(lane < GDN_HEADS, jax.nn.sigmoid(ba), g)


def _even_in(x, seq, nw, sc, sh, w_cat, conv_w, gate_params):
    t_rows = x.shape[0]
    tm = _row_tile(seq, 512)
    kern = functools.partial(_even_in_kernel, tm=tm, seq=seq)
    row = lambda w: pl.BlockSpec((tm, w), lambda i: (i, 0))
    full = lambda a: pl.BlockSpec(a.shape, lambda i: (0,) * a.ndim)
    outs = [jax.ShapeDtypeStruct((t_rows, GDN_WIDTH), F32)] * 5 + [jax.ShapeDtypeStruct((t_rows, LANES), F32)]
    return pl.pallas_call(
        kern,
        grid=(t_rows // tm,),
        in_specs=[row(D_MODEL),
                  pl.BlockSpec((8, D_MODEL), lambda i: (jnp.maximum(i * (tm // 8) - 1, 0), 0)),
                  full(nw), _mod_spec(tm, seq), _mod_spec(tm, seq), full(w_cat), full(conv_w),
                  full(gate_params)],
        out_specs=[row(GDN_WIDTH)] * 5 + [row(LANES)],
        out_shape=outs,
        scratch_shapes=[pltpu.VMEM((tm + 8, QKV_W), F32)],
        compiler_params=_cparams("parallel"),
        name="even_in",
    )(x, x, nw, sc, sh, w_cat, conv_w, gate_params)


def _gdn_kernel(q_ref, k_ref, v_ref, z_ref, gb_ref, nw_ref, ya_ref, state_scr, *, batch):
    @pl.when(pl.program_id(0) == 0)
    def _():
        state_scr[...] = jnp.zeros_like(state_scr)

    r64 = lax.broadcasted_iota(jnp.int32, (CHUNK, CHUNK), 0)
    c64 = lax.broadcasted_iota(jnp.int32, (CHUNK, CHUNK), 1)
    causal = r64 >= c64
    strict = r64 > c64
    tri = causal.astype(BF16)
    sel = (lax.broadcasted_iota(jnp.int32, (8, LANES), 1)
           == lax.broadcasted_iota(jnp.int32, (8, LANES), 0) + GDN_HEADS).astype(BF16)
    nw = nw_ref[...]
    units = [(b, hd) for b in range(batch) for hd in range(GDN_HEADS)]
    gbs = [gb_ref[b] for b in range(batch)]
    gcs = [sum(jnp.dot(tri, p, preferred_element_type=F32) for p in _split3(gb)) for gb in gbs]
    gcts = [sum(lax.dot_general(sel, p, (((1,), (1,)), ((), ())), preferred_element_type=F32)
                for p in _split3(gc)) for gc in gcs]

    qs, ks, gcols, egs, kbs = [], [], [], [], []
    for b, hd in units:
        sl = slice(hd * GDN_HEAD_DIM, (hd + 1) * GDN_HEAD_DIM)
        qs.append(q_ref[b, :, sl])
        ks.append(k_ref[b, :, sl])
        gcols.append(gcs[b][:, GDN_HEADS + hd:GDN_HEADS + hd + 1])
        egs.append(jnp.exp(gcols[-1]))
        kbs.append(ks[-1] * gbs[b][:, hd:hd + 1])
    kks = [_mm_nt(kb, kh) for kb, kh in zip(kbs, ks)]
    qks = [_mm_nt(qh, kh) for qh, kh in zip(qs, ks)]
    p_mats, sols = [], []
    for u, (b, hd) in enumerate(units):
        sl = slice(hd * GDN_HEAD_DIM, (hd + 1) * GDN_HEAD_DIM)
        decay = jnp.exp(jnp.where(causal, gcols[u] - gcts[b][hd:hd + 1, :], -1e30))
        p_mats.append(jnp.where(strict, -kks[u] * decay, 0.0))
        qks[u] = jnp.where(causal, qks[u] * decay, 0.0)
        sols.append(jnp.concatenate([v_ref[b, :, sl] * gbs[b][:, hd:hd + 1], kbs[u] * egs[u]], axis=1))
    n_fac = int(math.log2(CHUNK))
    for it in range(n_fac):
        sols = [sol + _mm_hi(p, sol) for p, sol in zip(p_mats, sols)]
        if it + 1 < n_fac:
            p_mats = [_mm_hi(p, p) for p in p_mats]
    sts = [state_scr[u] for u in range(len(units))]
    ws = [_mm(sol[:, GDN_HEAD_DIM:], st) for sol, st in zip(sols, sts)]
    os_ = [_mm(qh * eg, st) for qh, eg, st in zip(qs, egs, sts)]
    v_news = [sol[:, :GDN_HEAD_DIM] - w for sol, w in zip(sols, ws)]
    os_ = [o + _mm(qk, vn) for o, qk, vn in zip(os_, qks, v_news)]
    for u, (b, hd) in enumerate(units):
        sl = slice(hd * GDN_HEAD_DIM, (hd + 1) * GDN_HEAD_DIM)
        glast = gcols[u][CHUNK - 1:CHUNK, :]
        kd = ks[u] * jnp.exp(glast - gcols[u])
        state_scr[u] = sts[u] * jnp.exp(glast) + _mm_tn(kd, v_news[u])
        o = os_[u]
        on = o * lax.rsqrt(jnp.mean(o * o, axis=-1, keepdims=True) + EPS) * nw
        ya_ref[b, :, sl] = (on * _silu(z_ref[b, :, sl])).astype(ya_ref.dtype)


def _gdn(q, k, v, z, gb, norm_w, batch, seq):
    nc = seq // CHUNK
    t_rows = q.shape[0]
    blk = lambda w: pl.BlockSpec((batch, CHUNK, w), lambda c: (0, c, 0))
    r3 = lambda a: a.reshape(batch, seq, a.shape[-1])
    out = pl.pallas_call(
        functools.partial(_gdn_kernel, batch=batch),
        grid=(nc,),
        in_specs=[blk(GDN_WIDTH)] * 4 + [blk(LANES), pl.BlockSpec((1, GDN_HEAD_DIM), lambda c: (0, 0))],
        out_specs=blk(GDN_WIDTH),
        out_shape=jax.ShapeDtypeStruct((batch, seq, GDN_WIDTH), BF16),
        scratch_shapes=[pltpu.VMEM((batch * GDN_HEADS, GDN_HEAD_DIM, GDN_HEAD_DIM), F32)],
        compiler_params=_cparams("arbitrary"),
        name="gdn",
    )(r3(q), r3(k), r3(v), r3(z), r3(gb), norm_w)
    return out.reshape(t_rows, GDN_WIDTH)


def _s5_tables(lam_re, lam_im, log_step, b_re, b_im, c_re, c_im, n_lev):
    hp = lax.Precision.HIGHEST
    lr = jnp.minimum(lam_re.astype(F32), -1e-4)
    li = lam_im.astype(F32)
    dt = jnp.exp(log_step.astype(F32))[:, None]
    js = jnp.arange(S5_Q + 1, dtype=F32)[:, None, None]
    mag = jnp.exp(lr * dt * js)
    pr, pi = mag * jnp.cos(li * dt * js), mag * jnp.sin(li * dt * js)
    ar, ai = pr[1], pi[1]
    nr, ni = ar - 1.0, ai
    den = lr * lr + li * li
    cr, ci = (nr * lr + ni * li) / den, (ni * lr - nr * li) / den
    b_re, b_im = b_re.astype(F32), b_im.astype(F32)
    bbr = cr[..., None] * b_re - ci[..., None] * b_im
    bbi = cr[..., None] * b_im + ci[..., None] * b_re
    c_re, c_im = c_re.astype(F32), c_im.astype(F32)
    lbr = pr[..., None] * bbr - pi[..., None] * bbi
    lbi = pr[..., None] * bbi + pi[..., None] * bbr
    kern = (jnp.einsum('gon,jgni->gjoi', c_re, lbr[:S5_Q], precision=hp)
            - jnp.einsum('gon,jgni->gjoi', c_im, lbi[:S5_Q], precision=hp))
    s_idx = jnp.arange(S5_Q)[:, None]
    t_idx = jnp.arange(S5_Q)[None, :]
    lag = jnp.clip(t_idx - s_idx, 0, S5_Q - 1)
    toep = jnp.where((t_idx >= s_idx)[None, :, :, None, None], kern[:, lag], 0.0)
    toep = toep.transpose(0, 1, 4, 2, 3).reshape(S5_GROUPS, S5_Q * S5_GROUP, S5_Q * S5_GROUP)
    er = lbr[:S5_Q][::-1].transpose(1, 0, 3, 2).reshape(S5_GROUPS, S5_Q * S5_GROUP, S5_STATE)
    ei = lbi[:S5_Q][::-1].transpose(1, 0, 3, 2).reshape(S5_GROUPS, S5_Q * S5_GROUP, S5_STATE)
    e_mat = jnp.concatenate([er, ei], axis=-1)
    clr = c_re[None] * pr[1:, :, None, :] - c_im[None] * pi[1:, :, None, :]
    cli = c_re[None] * pi[1:, :, None, :] + c_im[None] * pr[1:, :, None, :]
    fr = clr.transpose(1, 3, 0, 2).reshape(S5_GROUPS, S5_STATE, S5_Q * S5_GROUP)
    fi = (-cli).transpose(1, 3, 0, 2).reshape(S5_GROUPS, S5_STATE, S5_Q * S5_GROUP)
    f_mat = jnp.concatenate([fr, fi], axis=1)
    a1, a2 = [], []
    cur_r, cur_i = pr[S5_Q], pi[S5_Q]
    for _ in range(n_lev):
        a1.append(jnp.concatenate([cur_r, cur_r], axis=-1))
        a2.append(jnp.concatenate([-cur_i, cur_i], axis=-1))
        cur_r, cur_i = cur_r * cur_r - cur_i * cur_i, 2.0 * cur_r * cur_i
    pad = [jnp.zeros_like(a1[0])] * (8 - n_lev % 8 if n_lev % 8 else 0)
    a1 = jnp.stack(a1 + pad, axis=1)
    a2 = jnp.stack(a2 + pad, axis=1)
    return toep.astype(BF16), e_mat.astype(BF16), f_mat.astype(BF16), a1, a2


def _s5_conv_kernel(u_ref, t_ref, e_ref, f_ref, a1_ref, a2_ref, y_ref, *, seg, n_lev):
    u = u_ref[0]
    y1 = jnp.dot(u, t_ref[0], preferred_element_type=F32)
    x = jnp.dot(u, e_ref[0], preferred_element_type=F32)
    rin = lax.broadcasted_iota(jnp.int32, x.shape, 0) % seg
    for lev in range(n_lev):
        s = 1 << lev
        xs = jnp.where(rin >= s, pltpu.roll(x, s, 0), 0.0)
        x = x + a1_ref[0, lev:lev + 1, :] * xs + a2_ref[0, lev:lev + 1, :] * pltpu.roll(xs, S5_STATE, 1)
    xp = jnp.where(rin >= 1, pltpu.roll(x, 1, 0), 0.0)
    y_ref[0] = y1 + _mm(xp, f_ref[0])


def _s5_conv(u_grp, toep, e_mat, f_mat, a1, a2, seg, n_lev):
    g, m, w = u_grp.shape
    kern = functools.partial(_s5_conv_kernel, seg=seg, n_lev=n_lev)
    per_g = lambda a: pl.BlockSpec((1,) + a.shape[1:], lambda i: (i, 0, 0))
    return pl.pallas_call(
        kern,
        grid=(g,),
        in_specs=[per_g(u_grp), per_g(toep), per_g(e_mat), per_g(f_mat), per_g(a1), per_g(a2)],
        out_specs=pl.BlockSpec((1, m, w), lambda i: (i, 0, 0)),
        out_shape=jax.ShapeDtypeStruct((g, m, w), F32),
        compiler_params=_cparams("parallel"),
        name="s5_conv",
    )(u_grp, toep, e_mat, f_mat, a1, a2)


def _s5_glu_kernel(y_ref, u_ref, d_ref, w_ref, b_ref, o_ref):
    y = y_ref[...] + d_ref[...] * u_ref[...]
    y = 0.5 * y * (1.0 + lax.erf(y * (2.0 ** -0.5)))
    ab = _mm(y, w_ref[...]) + b_ref[...]
    o_ref[...] = (ab[:, :S5_WIDTH] * jax.nn.sigmoid(ab[:, S5_WIDTH:])).astype(o_ref.dtype)


def _s5_glu(yconv, u, d_skip, glu_w, glu_b):
    t_rows = u.shape[0]
    tm = _row_tile(t_rows, 512)
    row = pl.BlockSpec((tm, S5_WIDTH), lambda i: (i, 0))
    full = lambda a: pl.BlockSpec(a.shape, lambda i: (0,) * a.ndim)
    return pl.pallas_call(
        _s5_glu_kernel,
        grid=(t_rows // tm,),
        in_specs=[row, row, full(d_skip), full(glu_w), full(glu_b)],
        out_specs=row,
        out_shape=jax.ShapeDtypeStruct((t_rows, S5_WIDTH), BF16),
        compiler_params=_cparams("parallel"),
        name="s5_glu",
    )(yconv, u, d_skip, glu_w, glu_b)


def _proj_res_kernel(*refs, n_in):
    x_ref, g_ref = refs[0], refs[1]
    ys = refs[2:2 + n_in]
    ws = refs[2 + n_in:2 + 2 * n_in]
    o_ref = refs[2 + 2 * n_in]
    acc = jnp.dot(ys[0][...], ws[0][...], preferred_element_type=F32)
    for y, w in zip(ys[1:], ws[1:]):
        acc = acc + jnp.dot(y[...], w[...], preferred_element_type=F32)
    o_ref[...] = x_ref[...] + g_ref[0] * acc


def _proj_res(x, gate, seq, ys, ws):
    t_rows = x.shape[0]
    tm = _row_tile(seq, 512)
    n_in = len(ys)
    full = lambda a: pl.BlockSpec(a.shape, lambda i: (0,) * a.ndim)
    return pl.pallas_call(
        functools.partial(_proj_res_kernel, n_in=n_in),
        grid=(t_rows // tm,),
        in_specs=([pl.BlockSpec((tm, D_MODEL), lambda i: (i, 0)), _mod_spec(tm, seq)]
                  + [pl.BlockSpec((tm, y.shape[1]), lambda i: (i, 0)) for y in ys]
                  + [full(w) for w in ws]),
        out_specs=pl.BlockSpec((tm, D_MODEL), lambda i: (i, 0)),
        out_shape=jax.ShapeDtypeStruct(x.shape, F32),
        compiler_params=_cparams("parallel"),
        name="proj_res",
    )(x, gate, *ys, *ws)


def _proj_res_t_kernel(x_ref, g_ref, yt_ref, w_ref, o_ref):
    acc = lax.dot_general(yt_ref[...], w_ref[...], (((0,), (0,)), ((), ())), preferred_element_type=F32)
    o_ref[...] = x_ref[...] + g_ref[0] * acc


def _proj_res_t(x, gate, seq, yt, w):
    t_rows = x.shape[0]
    tm = _row_tile(seq, 512)
    row = pl.BlockSpec((tm, D_MODEL), lambda i: (i, 0))
    return pl.pallas_call(
        _proj_res_t_kernel,
        grid=(t_rows // tm,),
        in_specs=[row, _mod_spec(tm, seq), pl.BlockSpec((yt.shape[0], tm), lambda i: (0, i)),
                  pl.BlockSpec(w.shape, lambda i: (0, 0))],
        out_specs=row,
        out_shape=jax.ShapeDtypeStruct(x.shape, F32),
        compiler_params=_cparams("parallel"),
        name="proj_res_t",
    )(x, gate, yt, w)


def _ffn_kernel(x_ref, nw_ref, sc_ref, sh_ref, g_ref, w1_ref, w3_ref, w2_ref, o_ref):
    x = x_ref[...]
    h = _norm_mod(x, nw_ref[...], sc_ref[0], sh_ref[0]).astype(BF16)
    a = jnp.dot(h, w1_ref[...], preferred_element_type=F32)
    b = jnp.dot(h, w3_ref[...], preferred_element_type=F32)
    f = jnp.dot((_silu(a) * b).astype(BF16), w2_ref[...], preferred_element_type=F32)
    o_ref[...] = x + g_ref[0] * f


def _ffn(x, seq, nw, sc, sh, gate, w13, w2):
    t_rows = x.shape[0]
    tm = _row_tile(seq, 256)
    row = pl.BlockSpec((tm, D_MODEL), lambda i: (i, 0))
    once = lambda a: pl.BlockSpec(a.shape, lambda i: (0,) * a.ndim, pipeline_mode=pl.Buffered(1))
    half = lambda j: pl.BlockSpec((D_MODEL, D_FF), lambda i: (0, j), pipeline_mode=pl.Buffered(1))
    return pl.pallas_call(
        _ffn_kernel,
        grid=(t_rows // tm,),
        in_specs=[row, once(nw), _mod_spec(tm, seq), _mod_spec(tm, seq), _mod_spec(tm, seq),
                  half(0), half(1), once(w2)],
        out_specs=row,
        out_shape=jax.ShapeDtypeStruct(x.shape, F32),
        compiler_params=_cparams("parallel"),
        name="ffn",
    )(x, nw, sc, sh, gate, w13, w13, w2)


LOG2E = 1.4426950408889634


def _odd_qkv_kernel(x_ref, nw_ref, sc_ref, sh_ref, w_ref, wvt_ref, qw_ref, kw_ref, q_ref, k_ref, vt_ref):
    h = _norm_mod(x_ref[...], nw_ref[...], sc_ref[0], sh_ref[0]).astype(BF16)
    proj = jnp.dot(h, w_ref[...], preferred_element_type=F32)
    r = lax.broadcasted_iota(jnp.int32, (LANES, LANES), 0) // DIFF_HEAD_DIM
    c = lax.broadcasted_iota(jnp.int32, (LANES, LANES), 1) // DIFF_HEAD_DIM
    grp = (r == c).astype(BF16)
    q_scale = DIFF_HEAD_DIM ** -0.5 * LOG2E
    for base, w_vec, scale, out in ((0, qw_ref, q_scale, q_ref), (D_MODEL, kw_ref, 1.0, k_ref)):
        for t in range(D_MODEL // LANES):
            xt = proj[:, base + t * LANES:base + (t + 1) * LANES]
            ss = sum(jnp.dot(p, grp, preferred_element_type=F32) for p in _split3(xt * xt))
            y = xt * lax.rsqrt(ss * (1.0 / DIFF_HEAD_DIM) + EPS) * w_vec[...]
            out[:, t * LANES:(t + 1) * LANES] = (y * scale).astype(out.dtype)
    vt_ref[...] = _mm_nt(wvt_ref[...], h).astype(vt_ref.dtype)


def _odd_qkv(x, seq, nw, sc, sh, w_qk, w_vt, qw, kw):
    t_rows = x.shape[0]
    tm = _row_tile(seq, 512)
    row = pl.BlockSpec((tm, D_MODEL), lambda i: (i, 0))
    full = lambda a: pl.BlockSpec(a.shape, lambda i: (0,) * a.ndim)
    out = jax.ShapeDtypeStruct((t_rows, D_MODEL), BF16)
    return pl.pallas_call(
        _odd_qkv_kernel,
        grid=(t_rows // tm,),
        in_specs=[row, full(nw), _mod_spec(tm, seq), _mod_spec(tm, seq), full(w_qk), full(w_vt), full(qw),
                  full(kw)],
        out_specs=[row, row, pl.BlockSpec((D_MODEL, tm), lambda i: (0, i))],
        out_shape=[out, out, jax.ShapeDtypeStruct((D_MODEL, t_rows), BF16)],
        compiler_params=_cparams("parallel"),
        name="odd_qkv",
    )(x, nw, sc, sh, w_qk, w_vt, qw, kw)


def _attn_kernel(bound_ref, q_ref, k_ref, vt_ref, lam_ref, sw_ref, o_ref, acc_scr, pp_scr, l_scr,
                 *, tq, tk, heads, out_scale):
    i = pl.program_id(2)
    lane = lax.broadcasted_iota(jnp.int32, (tq, LANES), 1)
    qs = []
    for hh in range(heads):
        q = q_ref[:, hh * LANES:(hh + 1) * LANES]
        zero = jnp.zeros_like(q)
        qs += [jnp.where(lane < DIFF_HEAD_DIM, q, zero), jnp.where(lane >= DIFF_HEAD_DIM, q, zero)]
    n_sub = tq // LANES
    n_full = i * (tq // tk)

    def load_k(j, hh):
        k0 = pl.multiple_of(j * tk, tk)
        return k0, k_ref[pl.ds(k0, tk), hh * LANES:(hh + 1) * LANES]

    def load_v(j, hh):
        return vt_ref[hh * LANES:(hh + 1) * LANES, pl.ds(pl.multiple_of(j * tk, tk), tk)]

    def scores(kb, u, c, k0, masked):
        s = lax.dot_general(kb, qs[u][c * LANES:(c + 1) * LANES, :], (((1,), (1,)), ((), ())),
                            preferred_element_type=F32)
        if masked:
            kc = (k0 + lax.broadcasted_iota(jnp.int32, (tk, LANES), 0)) // CHUNK
            qc = (i * tq + c * LANES + lax.broadcasted_iota(jnp.int32, (tk, LANES), 1)) // CHUNK
            s = jnp.where(kc <= qc, s, -1e30)
        return s

    def finish(ls):
        lam = lam_ref[0:1, 0:1]
        for hh in range(heads):
            o = acc_scr[2 * hh] / ls[2 * hh] - lam * (acc_scr[2 * hh + 1] / ls[2 * hh + 1])
            o = o * lax.rsqrt(jnp.mean(o * o, axis=0, keepdims=True) + EPS) * sw_ref[...] * out_scale
            o_ref[hh * LANES:(hh + 1) * LANES, :] = o.astype(o_ref.dtype)

    acc_scr[...] = jnp.zeros_like(acc_scr)

    @pl.when(bound_ref[0] == 1)
    def _():
        l_scr[...] = jnp.zeros_like(l_scr)
        pp_scr[1] = jnp.zeros(pp_scr.shape[1:], BF16)

        def values(j, hh):
            vb = load_v(j, hh)
            for u in (2 * hh, 2 * hh + 1):
                acc_scr[u] = acc_scr[u] + jnp.dot(vb, pp_scr[j % 2, u], preferred_element_type=F32)

        def visibility(jj, c):
            k_lo, k_hi = (jj * tk) // CHUNK, ((jj + 1) * tk - 1) // CHUNK
            q_lo, q_hi = (c * LANES) // CHUNK, ((c + 1) * LANES - 1) // CHUNK
            return "full" if k_hi <= q_lo else ("none" if k_lo > q_hi else "part")

        def step(j, jj):
            jp = jnp.maximum(j - 1, 0) + (j == 0).astype(jnp.int32)
            for hh in range(heads):
                values(jp, hh)
                k0, kb = load_k(j, hh)
                for u in (2 * hh, 2 * hh + 1):
                    for c in range(n_sub):
                        cs = slice(c * LANES, (c + 1) * LANES)
                        vis = "full" if jj is None else visibility(jj, c)
                        if vis == "none":
                            pp_scr[j % 2, u, :, cs] = jnp.zeros((tk, LANES), BF16)
                            continue
                        p = jnp.exp2(scores(kb, u, c, k0, vis == "part"))
                        l_scr[u, :, cs] = l_scr[u, :, cs] + jnp.sum(p.reshape(tk // 8, 8, LANES), axis=0)
                        pp_scr[j % 2, u, :, cs] = p.astype(BF16)

        def body(j, carry):
            step(j, None)
            return carry

        lax.fori_loop(0, n_full, body, 0)
        n_mask = tq // tk
        for jj in range(n_mask):
            step(n_full + jj, jj)
        for hh in range(heads):
            values(n_full + n_mask - 1, hh)
        finish([jnp.sum(l_scr[u], axis=0, keepdims=True) for u in range(2 * heads)])

    @pl.when(bound_ref[0] != 1)
    def _():
        neg = jnp.full((1, tq), -1e30, F32)
        zer = jnp.zeros((1, tq), F32)

        def block(j, carry, masked):
            out = []
            for hh in range(heads):
                k0, kb = load_k(j, hh)
                vb = load_v(j, hh)
                alphas = []
                for u in (2 * hh, 2 * hh + 1):
                    m, l = carry[2 * u], carry[2 * u + 1]
                    m_parts, l_parts, a_parts = [], [], []
                    for c in range(n_sub):
                        cs = slice(c * LANES, (c + 1) * LANES)
                        s = scores(kb, u, c, k0, masked)
                        m_new = jnp.maximum(m[:, cs], jnp.max(s, axis=0, keepdims=True))
                        p = jnp.exp2(s - m_new)
                        alpha = jnp.exp2(m[:, cs] - m_new)
                        l_parts.append(alpha * l[:, cs] + jnp.sum(p, axis=0, keepdims=True))
                        pp_scr[0, u, :, cs] = p.astype(BF16)
                        m_parts.append(m_new)
                        a_parts.append(alpha)
                    out += [jnp.concatenate(m_parts, axis=1), jnp.concatenate(l_parts, axis=1)]
                    alphas.append(jnp.concatenate(a_parts, axis=1))
                for t, u in enumerate((2 * hh, 2 * hh + 1)):
                    acc_scr[u] = alphas[t] * acc_scr[u] + jnp.dot(vb, pp_scr[0, u], preferred_element_type=F32)
            return tuple(out)

        carry = lax.fori_loop(0, n_full, lambda j, c: block(j, c, False), (neg, zer) * (2 * heads))
        for jj in range(tq // tk):
            carry = block(n_full + jj, carry, True)
        finish([carry[2 * u + 1] for u in range(2 * heads)])


ATTN_HEADS_PER_STEP = 4
SCORE_BOUND_LOG2 = 64.0


def _attention(q, k, vt, score_bound, lam_vec, subln_col, batch, seq, out_scale):
    tq = _row_tile(seq, 512)
    tk = _row_tile(tq, 256)
    nq = seq // tq
    assert seq // tk >= 2
    hp = ATTN_HEADS_PER_STEP
    wid = hp * LANES
    kern = functools.partial(_attn_kernel, tq=tq, tk=tk, heads=hp, out_scale=out_scale)
    full = lambda a: pl.BlockSpec(a.shape, lambda b, h, i: (0,) * a.ndim)
    bounded = (score_bound <= SCORE_BOUND_LOG2).astype(jnp.int32).reshape(1)
    return pl.pallas_call(
        kern,
        grid=(batch, DIFF_HEADS // hp, nq),
        in_specs=[pl.BlockSpec(memory_space=pltpu.SMEM),
                  pl.BlockSpec((tq, wid), lambda b, h, i: (b * nq + i, h)),
                  pl.BlockSpec((seq, wid), lambda b, h, i: (b, h)),
                  pl.BlockSpec((wid, seq), lambda b, h, i: (h, b)),
                  full(lam_vec), full(subln_col)],
        out_specs=pl.BlockSpec((wid, tq), lambda b, h, i: (h, b * nq + i)),
        out_shape=jax.ShapeDtypeStruct(vt.shape, BF16),
        scratch_shapes=[pltpu.VMEM((2 * hp, LANES, tq), F32),
                        pltpu.VMEM((2, 2 * hp, tk, tq), BF16), pltpu.VMEM((2 * hp, 8, tq), F32)],
        compiler_params=_cparams("parallel", "parallel", "arbitrary"),
        name="diff_attn",
    )(bounded, q, k, vt, lam_vec, subln_col)


def _router_kernel(x_ref, nw_ref, sc_ref, sh_ref, whi_ref, wlo_ref, h_ref, idx_ref, gate_ref):
    h = _norm_mod(x_ref[...], nw_ref[...], sc_ref[0], sh_ref[0])
    h_ref[...] = h
    h_hi = h.astype(BF16)
    h_lo = (h - h_hi.astype(F32)).astype(BF16)
    logits = (jnp.dot(h_hi, whi_ref[...], preferred_element_type=F32)
              + jnp.dot(h_hi, wlo_ref[...], preferred_element_type=F32)
              + jnp.dot(h_lo, whi_ref[...], preferred_element_type=F32))
    lane = lax.broadcasted_iota(jnp.int32, logits.shape, 1)
    lane_f = lane.astype(F32)
    logits = jnp.where(lane < N_EXPERTS, logits, -jnp.inf)
    m1 = jnp.max(logits, axis=-1, keepdims=True)
    i1 = jnp.min(jnp.where(logits == m1, lane_f, float(LANES)), axis=-1, keepdims=True)
    rest = jnp.where(lane_f == i1, -jnp.inf, logits)
    m2 = jnp.max(rest, axis=-1, keepdims=True)
    i2 = jnp.min(jnp.where(rest == m2, lane_f, float(LANES)), axis=-1, keepdims=True)
    e = jnp.exp(m2 - m1)
    g1 = 1.0 / (1.0 + e)
    idx_ref[...] = jnp.where(lane == 0, i1, jnp.where(lane == 1, i2, 0.0)).astype(jnp.int32)
    gate_ref[...] = jnp.where(lane == 0, g1, jnp.where(lane == 1, e * g1, 0.0))


def _router(x, seq, nw, sc, sh, w_hi, w_lo):
    t_rows = x.shape[0]
    tm = _row_tile(seq, 512)
    row = lambda w: pl.BlockSpec((tm, w), lambda i: (i, 0))
    full = lambda a: pl.BlockSpec(a.shape, lambda i: (0,) * a.ndim)
    return pl.pallas_call(
        _router_kernel,
        grid=(t_rows // tm,),
        in_specs=[row(D_MODEL), full(nw), _mod_spec(tm, seq), _mod_spec(tm, seq), full(w_hi), full(w_lo)],
        out_specs=[row(D_MODEL), row(LANES), row(LANES)],
        out_shape=[jax.ShapeDtypeStruct((t_rows, D_MODEL), F32),
                   jax.ShapeDtypeStruct((t_rows, LANES), jnp.int32),
                   jax.ShapeDtypeStruct((t_rows, LANES), F32)],
        compiler_params=_cparams("parallel"),
        name="router",
    )(x, nw, sc, sh, w_hi, w_lo)


GATHER_ROWS = 512


def _gather_kernel(idx_ref, src_hbm, out_hbm, sem):
    i = pl.program_id(0)
    base = i * GATHER_ROWS

    def start(r2, _):
        for prio in range(2):
            r = 2 * r2 + prio
            pltpu.make_async_copy(src_hbm.at[pl.ds(idx_ref[0, 0, r], 1), :],
                                  out_hbm.at[pl.ds(base + r, 1), :], sem).start(priority=prio)
        return 0

    lax.fori_loop(0, GATHER_ROWS // 2, start, 0, unroll=4)

    def wait_block():
        pltpu.make_async_copy(src_hbm.at[pl.ds(0, GATHER_ROWS), :], out_hbm.at[pl.ds(0, GATHER_ROWS), :],
                              sem).wait()

    @pl.when(i > 0)
    def _():
        wait_block()

    @pl.when(i == pl.num_programs(0) - 1)
    def _():
        wait_block()


def _gather_rows(src, idx):
    n = idx.shape[0]
    width = src.shape[1]
    return pl.pallas_call(
        _gather_kernel,
        grid=(n // GATHER_ROWS,),
        in_specs=[pl.BlockSpec((1, 1, GATHER_ROWS), lambda i: (i, 0, 0), memory_space=pltpu.SMEM),
                  pl.BlockSpec(memory_space=pl.ANY)],
        out_specs=pl.BlockSpec(memory_space=pl.ANY),
        scratch_shapes=[pltpu.SemaphoreType.DMA(())],
        out_shape=jax.ShapeDtypeStruct((n, width), src.dtype),
        compiler_params=_cparams("arbitrary"),
        name="gather_rows",
    )(idx.reshape(n // GATHER_ROWS, 1, GATHER_ROWS), src)


def _expert_kernel(te_ref, nt_ref, x_ref, w1_ref, w3_ref, w2_ref, o_ref):
    i = pl.program_id(0)

    @pl.when(i < nt_ref[0])
    def _():
        h = x_ref[...].astype(BF16)
        a = jnp.dot(h, w1_ref[0], preferred_element_type=F32)
        b = jnp.dot(h, w3_ref[0], preferred_element_type=F32)
        o_ref[...] = jnp.dot((_silu(a) * b).astype(BF16), w2_ref[0], preferred_element_type=F32)

    @pl.when(i >= nt_ref[0])
    def _():
        o_ref[...] = jnp.zeros_like(o_ref)


EXPERT_TM = 256


def _expert_ffn(x_perm, tile_expert, n_tiles_used, w13, w2):
    p_rows = x_perm.shape[0]
    tm = EXPERT_TM
    return pl.pallas_call(
        _expert_kernel,
        grid_spec=pltpu.PrefetchScalarGridSpec(
            num_scalar_prefetch=2,
            grid=(p_rows // tm,),
            in_specs=[pl.BlockSpec((tm, D_MODEL), lambda i, te, nt: (i, 0)),
                      pl.BlockSpec((1, D_MODEL, D_FF), lambda i, te, nt: (te[i], 0, 0)),
                      pl.BlockSpec((1, D_MODEL, D_FF), lambda i, te, nt: (te[i], 0, 1)),
                      pl.BlockSpec((1, D_FF, D_MODEL), lambda i, te, nt: (te[i], 0, 0))],
            out_specs=pl.BlockSpec((tm, D_MODEL), lambda i, te, nt: (i, 0)),
        ),
        out_shape=jax.ShapeDtypeStruct((p_rows, D_MODEL), F32),
        compiler_params=_cparams("arbitrary"),
        name="expert_ffn",
    )(tile_expert, n_tiles_used, x_perm, w13, w13, w2)


def _combine_kernel(x_ref, g_ref, gate_ref, y0_ref, y1_ref, o_ref):
    gates = gate_ref[...]
    f = gates[:, 0:1] * y0_ref[...] + gates[:, 1:2] * y1_ref[...]
    o_ref[...] = x_ref[...] + g_ref[0] * f


def _combine(x, gate_mod, seq, gates, y_tok):
    t_rows = x.shape[0]
    tm = _row_tile(seq, 512)
    nt = t_rows // tm
    row = pl.BlockSpec((tm, D_MODEL), lambda i: (i, 0))
    return pl.pallas_call(
        _combine_kernel,
        grid=(nt,),
        in_specs=[row, _mod_spec(tm, seq), pl.BlockSpec((tm, LANES), lambda i: (i, 0)),
                  row, pl.BlockSpec((tm, D_MODEL), lambda i: (i + nt, 0))],
        out_specs=row,
        out_shape=jax.ShapeDtypeStruct(x.shape, F32),
        compiler_params=_cparams("parallel"),
        name="moe_combine",
    )(x, gate_mod, gates, y_tok, y_tok)


def _dispatch_plan(expert_idx, t_rows):
    tm = EXPERT_TM
    flat = expert_idx.reshape(-1)
    onehot = (flat[:, None] == jnp.arange(N_EXPERTS)[None, :]).astype(jnp.int32)
    rank = jnp.take_along_axis(jnp.cumsum(onehot, axis=0) - onehot, flat[:, None], axis=1)[:, 0]
    counts = jnp.sum(onehot, axis=0)
    tiles_per = (counts + tm - 1) // tm
    tile_end = jnp.cumsum(tiles_per)
    offs = (tile_end - tiles_per) * tm
    pos = offs[flat] + rank
    n_tiles = (2 * t_rows) // tm + N_EXPERTS
    p_rows = n_tiles * tm
    tok_of_slot = jnp.zeros((p_rows,), jnp.int32).at[pos].set(jnp.arange(2 * t_rows, dtype=jnp.int32) // 2)
    tile_expert = jnp.minimum(
        jnp.sum(jnp.arange(n_tiles, dtype=jnp.int32)[:, None] >= tile_end[None, :], axis=1), N_EXPERTS - 1
    ).astype(jnp.int32)
    n_used = tile_end[-1:].astype(jnp.int32)
    pos2 = pos.reshape(t_rows, 2)
    gather_back = jnp.concatenate([pos2[:, 0], pos2[:, 1]]).astype(jnp.int32)
    return tok_of_slot, tile_expert, n_used, gather_back


def _mods(mod, layer, batch):
    m = mod[layer, :batch].reshape(batch, 6, 1, D_MODEL)
    return [m[:, j] for j in range(6)]


def _even_layer(x, batch, seq, mods, nw_mix, nw_ffn, w_in, conv_w, a_log, dt_bias, gdn_norm_w,
                lam_re, lam_im, log_step, b_re, b_im, c_re, c_im, d_skip, glu_w, glu_b, w_out, w13, w2):
    sh1, sc1, g1, sh2, sc2, g2 = mods
    t_rows = batch * seq
    qkvz, rest = w_in[:, :4 * GDN_WIDTH], w_in[:, 4 * GDN_WIDTH:]
    w_ba, w_u = rest[:, :2 * GDN_HEADS], rest[:, 2 * GDN_HEADS:]
    w_cat = jnp.concatenate(
        [qkvz, w_u, w_ba, jnp.zeros((D_MODEL, LANES - 2 * GDN_HEADS), w_in.dtype)], axis=1).astype(BF16)
    gate_params = jnp.zeros((8, LANES), F32)
    gate_params = gate_params.at[0, GDN_HEADS:2 * GDN_HEADS].set(a_log.astype(F32))
    gate_params = gate_params.at[1, GDN_HEADS:2 * GDN_HEADS].set(dt_bias.astype(F32))
    q, k, v, z, u, gb = _even_in(x, seq, nw_mix, sc1, sh1, w_cat, conv_w.astype(F32), gate_params)
    y_a = _gdn(q, k, v, z, gb, gdn_norm_w.reshape(1, GDN_HEAD_DIM).astype(F32), batch, seq)

    seg = seq // S5_Q
    n_lev = max(int(math.ceil(math.log2(seg))), 0)
    toep, e_mat, f_mat, a1, a2 = _s5_tables(lam_re, lam_im, log_step, b_re, b_im, c_re, c_im, max(n_lev, 1))
    m_rows = t_rows // S5_Q
    u_grp = (u.reshape(m_rows, S5_Q, S5_GROUPS, S5_GROUP).transpose(2, 0, 1, 3)
             .reshape(S5_GROUPS, m_rows, S5_Q * S5_GROUP).astype(BF16))
    yg = _s5_conv(u_grp, toep, e_mat, f_mat, a1, a2, seg, n_lev)
    yconv = (yg.reshape(S5_GROUPS, m_rows, S5_Q, S5_GROUP).transpose(1, 2, 0, 3)
             .reshape(t_rows, S5_WIDTH))
    y_b = _s5_glu(yconv, u, d_skip.reshape(1, S5_WIDTH).astype(F32), glu_w.astype(BF16),
                  glu_b.reshape(1, 2 * S5_WIDTH).astype(F32))
    wo = w_out.astype(BF16)
    x = _proj_res(x, g1, seq, [y_a, y_b], [wo[:GDN_WIDTH], wo[GDN_WIDTH:]])
    return _ffn(x, seq, nw_ffn, sc2, sh2, g2, w13.astype(BF16), w2.astype(BF16))


def _odd_layer(x, batch, seq, mods, nw_mix, nw_ffn, w_qkv, q_norm_w, k_norm_w, lq1, lk1, lq2, lk2,
               subln_w, w_out, router_w, w13, w2, lambda_init):
    sh1, sc1, g1, sh2, sc2, g2 = mods
    t_rows = batch * seq
    tile2 = lambda w: jnp.tile(w.astype(F32), LANES // DIFF_HEAD_DIM).reshape(1, LANES)
    w_qkv_b = w_qkv.astype(BF16)
    q, k, vt = _odd_qkv(x, seq, nw_mix, sc1, sh1, w_qkv_b[:, :2 * D_MODEL], w_qkv_b[:, 2 * D_MODEL:].T,
                        tile2(q_norm_w), tile2(k_norm_w))
    lam = (jnp.exp(jnp.sum(lq1.astype(F32) * lk1.astype(F32)))
           - jnp.exp(jnp.sum(lq2.astype(F32) * lk2.astype(F32))) + lambda_init)
    lam_vec = jnp.full((8, LANES), lam, F32)
    score_bound = (1.02 * LOG2E * DIFF_HEAD_DIM ** 0.5
                   * jnp.max(jnp.abs(q_norm_w.astype(F32))) * jnp.max(jnp.abs(k_norm_w.astype(F32))))
    ot = _attention(q, k, vt, score_bound, lam_vec, subln_w.reshape(LANES, 1).astype(F32), batch, seq,
                    1.0 - lambda_init)
    x = _proj_res_t(x, g1, seq, ot, w_out.astype(BF16))

    rw = jnp.zeros((D_MODEL, LANES), F32).at[:, :N_EXPERTS].set(router_w.astype(F32))
    rw_hi = rw.astype(BF16)
    rw_lo = (rw - rw_hi.astype(F32)).astype(BF16)
    h, idx, gates = _router(x, seq, nw_ffn, sc2, sh2, rw_hi, rw_lo)
    tok_of_slot, tile_expert, n_used, gather_back = _dispatch_plan(idx[:, :2], t_rows)
    h_perm = _gather_rows(h, tok_of_slot)
    y_perm = _expert_ffn(h_perm, tile_expert, n_used, w13.astype(BF16), w2.astype(BF16))
    y_tok = _gather_rows(y_perm, gather_back)
    return _combine(x, g2, seq, gates, y_tok)


def kernel(x, c, ada_w, ada_b, norm_mix_w, norm_ffn_w, even_w_in, even_conv_w, even_a_log, even_dt_bias, even_gdn_norm_w, even_lam_re, even_lam_im, even_log_step, even_b_re, even_b_im, even_c_re, even_c_im, even_d_skip, even_glu_w, even_glu_b, even_w_out, even_ffn_w13, even_ffn_w2, odd_w_qkv, odd_q_norm_w, odd_k_norm_w, odd_lambda_q1, odd_lambda_k1, odd_lambda_q2, odd_lambda_k2, odd_subln_w, odd_w_out, odd_router_w, odd_expert_w13, odd_expert_w2):
    batch, seq, d = x.shape
    assert d == D_MODEL and seq % CHUNK == 0 and seq % S5_Q == 0
    c_pad = jnp.zeros((8, d), F32).at[:batch].set(c.astype(F32))
    mod = _adaln(c_pad, ada_w.astype(BF16), ada_b.astype(F32))
    xf = x.astype(F32).reshape(batch * seq, d)
    for layer in range(DEPTH):
        i = layer // 2
        mods = _mods(mod, layer, batch)
        nw_mix = norm_mix_w[layer].reshape(1, d).astype(F32)
        nw_ffn = norm_ffn_w[layer].reshape(1, d).astype(F32)
        if layer % 2 == 0:
            xf = _even_layer(xf, batch, seq, mods, nw_mix, nw_ffn, even_w_in[i], even_conv_w[i], even_a_log[i],
                             even_dt_bias[i], even_gdn_norm_w[i], even_lam_re[i], even_lam_im[i],
                             even_log_step[i], even_b_re[i], even_b_im[i], even_c_re[i], even_c_im[i],
                             even_d_skip[i], even_glu_w[i], even_glu_b[i], even_w_out[i],
                             even_ffn_w13[i], even_ffn_w2[i])
        else:
            lambda_init = 0.8 - 0.6 * math.exp(-0.3 * layer)
            xf = _odd_layer(xf, batch, seq, mods, nw_mix, nw_ffn, odd_w_qkv[i], odd_q_norm_w[i], odd_k_norm_w[i],
                            odd_lambda_q1[i], odd_lambda_k1[i], odd_lambda_q2[i], odd_lambda_k2[i],
                            odd_subln_w[i], odd_w_out[i], odd_router_w[i], odd_expert_w13[i],
                            odd_expert_w2[i], lambda_init)
    return xf.reshape(batch, seq, d).astype(x.dtype)
```

```python
import functools
import math

import jax
import jax.numpy as jnp
from jax import lax
from jax.experimental import pallas as pl
from jax.experimental.pallas import tpu as pltpu

F32 = jnp.float32
BF16 = jnp.bfloat16

D_MODEL = 1024
DEPTH = 4
CHUNK = 64
EPS = 1e-6
GDN_HEAD_DIM = 128
GDN_WIDTH = D_MODEL // 2
GDN_HEADS = GDN_WIDTH // GDN_HEAD_DIM
CONV_K = 4
QKV_W = 3 * GDN_WIDTH
S5_WIDTH = D_MODEL - GDN_WIDTH
S5_GROUP = 16
S5_GROUPS = S5_WIDTH // S5_GROUP
S5_STATE = 64
S5_Q = 16
DIFF_HEAD_DIM = 64
DIFF_HEADS = D_MODEL // (2 * DIFF_HEAD_DIM)
D_FF = ((8 * D_MODEL // 3 + 127) // 128) * 128
N_EXPERTS = 8
LANES = 128
EVEN_IN_COLS = QKV_W + GDN_WIDTH + S5_WIDTH + LANES
VMEM_LIMIT = 56 * 1024 * 1024


def _cparams(*sem):
    return pltpu.CompilerParams(dimension_semantics=sem, vmem_limit_bytes=VMEM_LIMIT)


def _mm(a, b):
    return jnp.dot(a.astype(BF16), b.astype(BF16), preferred_element_type=F32)


def _mm_nt(a, b):
    return lax.dot_general(a.astype(BF16), b.astype(BF16), (((1,), (1,)), ((), ())),
                           preferred_element_type=F32)


def _mm_tn(a, b):
    return lax.dot_general(a.astype(BF16), b.astype(BF16), (((0,), (0,)), ((), ())),
                           preferred_element_type=F32)


def _mm_hi(a, b):
    a_hi = a.astype(BF16)
    a_lo = (a - a_hi.astype(F32)).astype(BF16)
    b_hi = b.astype(BF16)
    b_lo = (b - b_hi.astype(F32)).astype(BF16)
    return (jnp.dot(a_hi, b_hi, preferred_element_type=F32)
            + (jnp.dot(a_hi, b_lo, preferred_element_type=F32) + jnp.dot(a_lo, b_hi, preferred_element_type=F32)))


def _split3(x):
    hi = x.astype(BF16)
    r = x - hi.astype(F32)
    mid = r.astype(BF16)
    lo = (r - mid.astype(F32)).astype(BF16)
    return hi, mid, lo


def _norm_mod(x, nw, sc, sh):
    ms = jnp.mean(x * x, axis=-1, keepdims=True)
    return (x * lax.rsqrt(ms + EPS) * nw) * (1.0 + sc) + sh


def _silu(x):
    return x * jax.nn.sigmoid(x)


def _row_tile(n, pref):
    t = min(pref, n)
    while n % t:
        t //= 2
    return t


def _adaln_kernel(c_ref, w_ref, b_ref, o_ref):
    c = c_ref[...]
    o_ref[0] = _mm(_silu(c), w_ref[0]) + b_ref[0]


def _adaln(c_pad, ada_w, ada_b):
    depth, d, n = ada_w.shape
    tn = 1536
    rows = c_pad.shape[0]
    return pl.pallas_call(
        _adaln_kernel,
        grid=(depth, n // tn),
        in_specs=[pl.BlockSpec((rows, d), lambda l, j: (0, 0)),
                  pl.BlockSpec((1, d, tn), lambda l, j: (l, 0, j)),
                  pl.BlockSpec((1, 1, tn), lambda l, j: (l, 0, j))],
        out_specs=pl.BlockSpec((1, rows, tn), lambda l, j: (l, 0, j)),
        out_shape=jax.ShapeDtypeStruct((depth, rows, n), F32),
        compiler_params=_cparams("parallel", "parallel"),
        name="adaln",
    )(c_pad, ada_w, ada_b.reshape(depth, 1, n))


def _mod_spec(tm, seq):
    return pl.BlockSpec((1, 1, D_MODEL), lambda i, *_: ((i * tm) // seq, 0, 0))


def _even_in_kernel(x_ref, xh_ref, nw_ref, sc_ref, sh_ref, w_ref, cw_ref, gp_ref,
                    q_ref, k_ref, v_ref, z_ref, u_ref, gb_ref, pre_scr, *, tm, seq):
    i = pl.program_id(0)
    nw = nw_ref[...]
    sc = sc_ref[0]
    sh = sh_ref[0]
    h = _norm_mod(x_ref[...], nw, sc, sh).astype(BF16)
    proj = jnp.dot(h, w_ref[...], preferred_element_type=F32)
    hh = _norm_mod(xh_ref[...], nw, sc, sh).astype(BF16)
    preh = jnp.dot(hh, w_ref[:, :QKV_W], preferred_element_type=F32)
    preh = jnp.where((i * tm) % seq == 0, 0.0, preh)
    pre = proj[:, :QKV_W]
    pre_scr[0:8, :] = preh
    pre_scr[8:8 + tm, :] = pre
    acc = pre * cw_ref[CONV_K - 1:CONV_K, :]
    for j in range(CONV_K - 1):
        acc = acc + pre_scr[pl.ds(8 - (CONV_K - 1) + j, tm), :] * cw_ref[j:j + 1, :]
    qkv = _silu(acc)
    for hd in range(GDN_HEADS):
        lo = hd * GDN_HEAD_DIM
        qh = qkv[:, lo:lo + GDN_HEAD_DIM]
        kh = qkv[:, GDN_WIDTH + lo:GDN_WIDTH + lo + GDN_HEAD_DIM]
        q_ref[:, lo:lo + GDN_HEAD_DIM] = (
            qh * lax.rsqrt(jnp.sum(qh * qh, axis=-1, keepdims=True) + EPS) * GDN_HEAD_DIM ** -0.5)
        k_ref[:, lo:lo + GDN_HEAD_DIM] = kh * lax.rsqrt(jnp.sum(kh * kh, axis=-1, keepdims=True) + EPS)
    v_ref[...] = qkv[:, 2 * GDN_WIDTH:]
    z_ref[...] = proj[:, QKV_W:QKV_W + GDN_WIDTH]
    u_ref[...] = proj[:, QKV_W + GDN_WIDTH:QKV_W + GDN_WIDTH + S5_WIDTH]
    ba = proj[:, QKV_W + GDN_WIDTH + S5_WIDTH:]
    lane = lax.broadcasted_iota(jnp.int32, ba.shape, 1)
    t = ba + gp_ref[1:2, :]
    softplus = jnp.maximum(t, 0.0) + jnp.log1p(jnp.exp(-jnp.abs(t)))
    g = -jnp.exp(gp_ref[0:1, :]) * softplus
    gb_ref[...] = jnp.where(lane < GDN_HEADS, jax.nn.sigmoid(ba), g)


def _even_in(x, seq, nw, sc, sh, w_cat, conv_w, gate_params):
    t_rows = x.shape[0]
    tm = _row_tile(seq, 512)
    kern = functools.partial(_even_in_kernel, tm=tm, seq=seq)
    row = lambda w: pl.BlockSpec((tm, w), lambda i: (i, 0))
    full = lambda a: pl.BlockSpec(a.shape, lambda i: (0,) * a.ndim)
    outs = [jax.ShapeDtypeStruct((t_rows, GDN_WIDTH), F32)] * 5 + [jax.ShapeDtypeStruct((t_rows, LANES), F32)]
    return pl.pallas_call(
        kern,
        grid=(t_rows // tm,),
        in_specs=[row(D_MODEL),
                  pl.BlockSpec((8, D_MODEL), lambda i: (jnp.maximum(i * (tm // 8) - 1, 0), 0)),
                  full(nw), _mod_spec(tm, seq), _mod_spec(tm, seq), full(w_cat), full(conv_w),
                  full(gate_params)],
        out_specs=[row(GDN_WIDTH)] * 5 + [row(LANES)],
        out_shape=outs,
        scratch_shapes=[pltpu.VMEM((tm + 8, QKV_W), F32)],
        compiler_params=_cparams("parallel"),
        name="even_in",
    )(x, x, nw, sc, sh, w_cat, conv_w, gate_params)


def _gdn_kernel(q_ref, k_ref, v_ref, z_ref, gb_ref, nw_ref, ya_ref, state_scr, *, batch):
    @pl.when(pl.program_id(0) == 0)
    def _():
        state_scr[...] = jnp.zeros_like(state_scr)

    r64 = lax.broadcasted_iota(jnp.int32, (CHUNK, CHUNK), 0)
    c64 = lax.broadcasted_iota(jnp.int32, (CHUNK, CHUNK), 1)
    causal = r64 >= c64
    strict = r64 > c64
    tri = causal.astype(BF16)
    sel = (lax.broadcasted_iota(jnp.int32, (8, LANES), 1)
           == lax.broadcasted_iota(jnp.int32, (8, LANES), 0) + GDN_HEADS).astype(BF16)
    nw = nw_ref[...]
    units = [(b, hd) for b in range(batch) for hd in range(GDN_HEADS)]
    gbs = [gb_ref[b] for b in range(batch)]
    gcs = [sum(jnp.dot(tri, p, preferred_element_type=F32) for p in _split3(gb)) for gb in gbs]
    gcts = [sum(lax.dot_general(sel, p, (((1,), (1,)), ((), ())), preferred_element_type=F32)
                for p in _split3(gc)) for gc in gcs]

    qs, ks, gcols, egs, kbs = [], [], [], [], []
    for b, hd in units:
        sl = slice(hd * GDN_HEAD_DIM, (hd + 1) * GDN_HEAD_DIM)
        qs.append(q_ref[b, :, sl])
        ks.append(k_ref[b, :, sl])
        gcols.append(gcs[b][:, GDN_HEADS + hd:GDN_HEADS + hd + 1])
        egs.append(jnp.exp(gcols[-1]))
        kbs.append(ks[-1] * gbs[b][:, hd:hd + 1])
    kks = [_mm_nt(kb, kh) for kb, kh in zip(kbs, ks)]
    qks = [_mm_nt(qh, kh) for qh, kh in zip(qs, ks)]
    p_mats, sols = [], []
    for u, (b, hd) in enumerate(units):
        sl = slice(hd * GDN_HEAD_DIM, (hd + 1) * GDN_HEAD_DIM)
        decay = jnp.exp(jnp.where(causal, gcols[u] - gcts[b][hd:hd + 1, :], -1e30))
        p_mats.append(jnp.where(strict, -kks[u] * decay, 0.0))
        qks[u] = jnp.where(causal, qks[u] * decay, 0.0)
        sols.append(jnp.concatenate([v_ref[b, :, sl] * gbs[b][:, hd:hd + 1], kbs[u] * egs[u]], axis=1))
    n_fac = int(math.log2(CHUNK))
    for it in range(n_fac):
        sols = [sol + _mm_hi(p, sol) for p, sol in zip(p_mats, sols)]
        if it + 1 < n_fac:
            p_mats = [_mm_hi(p, p) for p in p_mats]
    sts = [state_scr[u] for u in range(len(units))]
    ws = [_mm(sol[:, GDN_HEAD_DIM:], st) for sol, st in zip(sols, sts)]
    os_ = [_mm(qh * eg, st) for qh, eg, st in zip(qs, egs, sts)]
    v_news = [sol[:, :GDN_HEAD_DIM] - w for sol, w in zip(sols, ws)]
    os_ = [o + _mm(qk, vn) for o, qk, vn in zip(os_, qks, v_news)]
    for u, (b, hd) in enumerate(units):
        sl = slice(hd * GDN_HEAD_DIM, (hd + 1) * GDN_HEAD_DIM)
        glast = gcols[u][CHUNK - 1:CHUNK, :]
        kd = ks[u] * jnp.exp(glast - gcols[u])
        state_scr[u] = sts[u] * jnp.exp(glast) + _mm_tn(kd, v_news[u])
        o = os_[u]
        on = o * lax.rsqrt(jnp.mean(o * o, axis=-1, keepdims=True) + EPS) * nw
        ya_ref[b, :, sl] = (on * _silu(z_ref[b, :, sl])).astype(ya_ref.dtype)


def _gdn(q, k, v, z, gb, norm_w, batch, seq):
    nc = seq // CHUNK
    t_rows = q.shape[0]
    blk = lambda w: pl.BlockSpec((batch, CHUNK, w), lambda c: (0, c, 0))
    r3 = lambda a: a.reshape(batch, seq, a.shape[-1])
    out = pl.pallas_call(
        functools.partial(_gdn_kernel, batch=batch),
        grid=(nc,),
        in_specs=[blk(GDN_WIDTH)] * 4 + [blk(LANES), pl.BlockSpec((1, GDN_HEAD_DIM), lambda c: (0, 0))],
        out_specs=blk(GDN_WIDTH),
        out_shape=jax.ShapeDtypeStruct((batch, seq, GDN_WIDTH), BF16),
        scratch_shapes=[pltpu.VMEM((batch * GDN_HEADS, GDN_HEAD_DIM, GDN_HEAD_DIM), F32)],
        compiler_params=_cparams("arbitrary"),
        name="gdn",
    )(r3(q), r3(k), r3(v), r3(z), r3(gb), norm_w)
    return out.reshape(t_rows, GDN_WIDTH)


def _s5_tables(lam_re, lam_im, log_step, b_re, b_im, c_re, c_im, n_lev):
    hp = lax.Precision.HIGHEST
    lr = jnp.minimum(lam_re.astype(F32), -1e-4)
    li = lam_im.astype(F32)
    dt = jnp.exp(log_step.astype(F32))[:, None]
    js = jnp.arange(S5_Q + 1, dtype=F32)[:, None, None]
    mag = jnp.exp(lr * dt * js)
    pr, pi = mag * jnp.cos(li * dt * js), mag * jnp.sin(li * dt * js)
    ar, ai = pr[1], pi[1]
    nr, ni = ar - 1.0, ai
    den = lr * lr + li * li
    cr, ci = (nr * lr + ni * li) / den, (ni * lr - nr * li) / den
    b_re, b_im = b_re.astype(F32), b_im.astype(F32)
    bbr = cr[..., None] * b_re - ci[..., None] * b_im
    bbi = cr[..., None] * b_im + ci[..., None] * b_re
    c_re, c_im = c_re.astype(F32), c_im.astype(F32)
    lbr = pr[..., None] * bbr - pi[..., None] * bbi
    lbi = pr[..., None] * bbi + pi[..., None] * bbr
    kern = (jnp.einsum('gon,jgni->gjoi', c_re, lbr[:S5_Q], precision=hp)
            - jnp.einsum('gon,jgni->gjoi', c_im, lbi[:S5_Q], precision=hp))
    s_idx = jnp.arange(S5_Q)[:, None]
    t_idx = jnp.arange(S5_Q)[None, :]
    lag = jnp.clip(t_idx - s_idx, 0, S5_Q - 1)
    toep = jnp.where((t_idx >= s_idx)[None, :, :, None, None], kern[:, lag], 0.0)
    toep = toep.transpose(0, 1, 4, 2, 3).reshape(S5_GROUPS, S5_Q * S5_GROUP, S5_Q * S5_GROUP)
    er = lbr[:S5_Q][::-1].transpose(1, 0, 3, 2).reshape(S5_GROUPS, S5_Q * S5_GROUP, S5_STATE)
    ei = lbi[:S5_Q][::-1].transpose(1, 0, 3, 2).reshape(S5_GROUPS, S5_Q * S5_GROUP, S5_STATE)
    e_mat = jnp.concatenate([er, ei], axis=-1)
    clr = c_re[None] * pr[1:, :, None, :] - c_im[None] * pi[1:, :, None, :]
    cli = c_re[None] * pi[1:, :, None, :] + c_im[None] * pr[1:, :, None, :]
    fr = clr.transpose(1, 3, 0, 2).reshape(S5_GROUPS, S5_STATE, S5_Q * S5_GROUP)
    fi = (-cli).transpose(1, 3, 0, 2).reshape(S5_GROUPS, S5_STATE, S5_Q * S5_GROUP)
    f_mat = jnp.concatenate([fr, fi], axis=1)
    a1, a2 = [], []
    cur_r, cur_i = pr[S5_Q], pi[S5_Q]
    for _ in range(n_lev):
        a1.append(jnp.concatenate([cur_r, cur_r], axis=-1))
        a2.append(jnp.concatenate([-cur_i, cur_i], axis=-1))
        cur_r, cur_i = cur_r * cur_r - cur_i * cur_i, 2.0 * cur_r * cur_i
    pad = [jnp.zeros_like(a1[0])] * (8 - n_lev % 8 if n_lev % 8 else 0)
    a1 = jnp.stack(a1 + pad, axis=1)
    a2 = jnp.stack(a2 + pad, axis=1)
    return toep.astype(BF16), e_mat.astype(BF16), f_mat.astype(BF16), a1, a2


def _s5_conv_kernel(u_ref, t_ref, e_ref, f_ref, a1_ref, a2_ref, y_ref, *, seg, n_lev):
    u = u_ref[0]
    y1 = jnp.dot(u, t_ref[0], preferred_element_type=F32)
    x = jnp.dot(u, e_ref[0], preferred_element_type=F32)
    rin = lax.broadcasted_iota(jnp.int32, x.shape, 0) % seg
    for lev in range(n_lev):
        s = 1 << lev
        xs = jnp.where(rin >= s, pltpu.roll(x, s, 0), 0.0)
        x = x + a1_ref[0, lev:lev + 1, :] * xs + a2_ref[0, lev:lev + 1, :] * pltpu.roll(xs, S5_STATE, 1)
    xp = jnp.where(rin >= 1, pltpu.roll(x, 1, 0), 0.0)
    y_ref[0] = y1 + _mm(xp, f_ref[0])


def _s5_conv(u_grp, toep, e_mat, f_mat, a1, a2, seg, n_lev):
    g, m, w = u_grp.shape
    kern = functools.partial(_s5_conv_kernel, seg=seg, n_lev=n_lev)
    per_g = lambda a: pl.BlockSpec((1,) + a.shape[1:], lambda i: (i, 0, 0))
    return pl.pallas_call(
        kern,
        grid=(g,),
        in_specs=[per_g(u_grp), per_g(toep), per_g(e_mat), per_g(f_mat), per_g(a1), per_g(a2)],
        out_specs=pl.BlockSpec((1, m, w), lambda i: (i, 0, 0)),
        out_shape=jax.ShapeDtypeStruct((g, m, w), F32),
        compiler_params=_cparams("parallel"),
        name="s5_conv",
    )(u_grp, toep, e_mat, f_mat, a1, a2)


def _s5_glu_kernel(y_ref, u_ref, d_ref, w_ref, b_ref, o_ref):
    y = y_ref[...] + d_ref[...] * u_ref[...]
    y = 0.5 * y * (1.0 + lax.erf(y * (2.0 ** -0.5)))
    ab = _mm(y, w_ref[...]) + b_ref[...]
    o_ref[...] = (ab[:, :S5_WIDTH] * jax.nn.sigmoid(ab[:, S5_WIDTH:])).astype(o_ref.dtype)


def _s5_glu(yconv, u, d_skip, glu_w, glu_b):
    t_rows = u.shape[0]
    tm = _row_tile(t_rows, 512)
    row = pl.BlockSpec((tm, S5_WIDTH), lambda i: (i, 0))
    full = lambda a: pl.BlockSpec(a.shape, lambda i: (0,) * a.ndim)
    return pl.pallas_call(
        _s5_glu_kernel,
        grid=(t_rows // tm,),
        in_specs=[row, row, full(d_skip), full(glu_w), full(glu_b)],
        out_specs=row,
        out_shape=jax.ShapeDtypeStruct((t_rows, S5_WIDTH), BF16),
        compiler_params=_cparams("parallel"),
        name="s5_glu",
    )(yconv, u, d_skip, glu_w, glu_b)


def _proj_res_kernel(*refs, n_in):
    x_ref, g_ref = refs[0], refs[1]
    ys = refs[2:2 + n_in]
    ws = refs[2 + n_in:2 + 2 * n_in]
    o_ref = refs[2 + 2 * n_in]
    acc = jnp.dot(ys[0][...], ws[0][...], preferred_element_type=F32)
    for y, w in zip(ys[1:], ws[1:]):
        acc = acc + jnp.dot(y[...], w[...], preferred_element_type=F32)
    o_ref[...] = x_ref[...] + g_ref[0] * acc


def _proj_res(x, gate, seq, ys, ws):
    t_rows = x.shape[0]
    tm = _row_tile(seq, 512)
    n_in = len(ys)
    full = lambda a: pl.BlockSpec(a.shape, lambda i: (0,) * a.ndim)
    return pl.pallas_call(
        functools.partial(_proj_res_kernel, n_in=n_in),
        grid=(t_rows // tm,),
        in_specs=([pl.BlockSpec((tm, D_MODEL), lambda i: (i, 0)), _mod_spec(tm, seq)]
                  + [pl.BlockSpec((tm, y.shape[1]), lambda i: (i, 0)) for y in ys]
                  + [full(w) for w in ws]),
        out_specs=pl.BlockSpec((tm, D_MODEL), lambda i: (i, 0)),
        out_shape=jax.ShapeDtypeStruct(x.shape, F32),
        compiler_params=_cparams("parallel"),
        name="proj_res",
    )(x, gate, *ys, *ws)


def _proj_res_t_kernel(x_ref, g_ref, yt_ref, w_ref, o_ref):
    acc = lax.dot_general(yt_ref[...], w_ref[...], (((0,), (0,)), ((), ())), preferred_element_type=F32)
    o_ref[...] = x_ref[...] + g_ref[0] * acc


def _proj_res_t(x, gate, seq, yt, w):
    t_rows = x.shape[0]
    tm = _row_tile(seq, 512)
    row = pl.BlockSpec((tm, D_MODEL), lambda i: (i, 0))
    return pl.pallas_call(
        _proj_res_t_kernel,
        grid=(t_rows // tm,),
        in_specs=[row, _mod_spec(tm, seq), pl.BlockSpec((yt.shape[0], tm), lambda i: (0, i)),
                  pl.BlockSpec(w.shape, lambda i: (0, 0))],
        out_specs=row,
        out_shape=jax.ShapeDtypeStruct(x.shape, F32),
        compiler_params=_cparams("parallel"),
        name="proj_res_t",
    )(x, gate, yt, w)


def _ffn_kernel(x_ref, nw_ref, sc_ref, sh_ref, g_ref, w1_ref, w3_ref, w2_ref, o_ref):
    x = x_ref[...]
    h = _norm_mod(x, nw_ref[...], sc_ref[0], sh_ref[0]).astype(BF16)
    a = jnp.dot(h, w1_ref[...], preferred_element_type=F32)
    b = jnp.dot(h, w3_ref[...], preferred_element_type=F32)
    f = jnp.dot((_silu(a) * b).astype(BF16), w2_ref[...], preferred_element_type=F32)
    o_ref[...] = x + g_ref[0] * f


def _ffn(x, seq, nw, sc, sh, gate, w13, w2):
    t_rows = x.shape[0]
    tm = _row_tile(seq, 256)
    row = pl.BlockSpec((tm, D_MODEL), lambda i: (i, 0))
    once = lambda a: pl.BlockSpec(a.shape, lambda i: (0,) * a.ndim, pipeline_mode=pl.Buffered(1))
    half = lambda j: pl.BlockSpec((D_MODEL, D_FF), lambda i: (0, j), pipeline_mode=pl.Buffered(1))
    return pl.pallas_call(
        _ffn_kernel,
        grid=(t_rows // tm,),
        in_specs=[row, once(nw), _mod_spec(tm, seq), _mod_spec(tm, seq), _mod_spec(tm, seq),
                  half(0), half(1), once(w2)],
        out_specs=row,
        out_shape=jax.ShapeDtypeStruct(x.shape, F32),
        compiler_params=_cparams("parallel"),
        name="ffn",
    )(x, nw, sc, sh, gate, w13, w13, w2)


LOG2E = 1.4426950408889634


def _odd_qkv_kernel(x_ref, nw_ref, sc_ref, sh_ref, w_ref, wvt_ref, qw_ref, kw_ref, q_ref, k_ref, vt_ref):
    h = _norm_mod(x_ref[...], nw_ref[...], sc_ref[0], sh_ref[0]).astype(BF16)
    proj = jnp.dot(h, w_ref[...], preferred_element_type=F32)
    r = lax.broadcasted_iota(jnp.int32, (LANES, LANES), 0) // DIFF_HEAD_DIM
    c = lax.broadcasted_iota(jnp.int32, (LANES, LANES), 1) // DIFF_HEAD_DIM
    grp = (r == c).astype(BF16)
    q_scale = DIFF_HEAD_DIM ** -0.5 * LOG2E
    for base, w_vec, scale, out in ((0, qw_ref, q_scale, q_ref), (D_MODEL, kw_ref, 1.0, k_ref)):
        for t in range(D_MODEL // LANES):
            xt = proj[:, base + t * LANES:base + (t + 1) * LANES]
            ss = sum(jnp.dot(p, grp, preferred_element_type=F32) for p in _split3(xt * xt))
            y = xt * lax.rsqrt(ss * (1.0 / DIFF_HEAD_DIM) + EPS) * w_vec[...]
            out[:, t * LANES:(t + 1) * LANES] = (y * scale).astype(out.dtype)
    vt_ref[...] = _mm_nt(wvt_ref[...], h).astype(vt_ref.dtype)


def _odd_qkv(x, seq, nw, sc, sh, w_qk, w_vt, qw, kw):
    t_rows = x.shape[0]
    tm = _row_tile(seq, 512)
    row = pl.BlockSpec((tm, D_MODEL), lambda i: (i, 0))
    full = lambda a: pl.BlockSpec(a.shape, lambda i: (0,) * a.ndim)
    out = jax.ShapeDtypeStruct((t_rows, D_MODEL), BF16)
    return pl.pallas_call(
        _odd_qkv_kernel,
        grid=(t_rows // tm,),
        in_specs=[row, full(nw), _mod_spec(tm, seq), _mod_spec(tm, seq), full(w_qk), full(w_vt), full(qw),
                  full(kw)],
        out_specs=[row, row, pl.BlockSpec((D_MODEL, tm), lambda i: (0, i))],
        out_shape=[out, out, jax.ShapeDtypeStruct((D_MODEL, t_rows), BF16)],
        compiler_params=_cparams("parallel"),
        name="odd_qkv",
    )(x, nw, sc, sh, w_qk, w_vt, qw, kw)


def _attn_kernel(bound_ref, q_ref, k_ref, vt_ref, lam_ref, sw_ref, o_ref, acc_scr, pp_scr, l_scr,
                 *, tq, tk, heads, out_scale):
    i = pl.program_id(2)
    lane = lax.broadcasted_iota(jnp.int32, (tq, LANES), 1)
    qs = []
    for hh in range(heads):
        q = q_ref[:, hh * LANES:(hh + 1) * LANES]
        zero = jnp.zeros_like(q)
        qs += [jnp.where(lane < DIFF_HEAD_DIM, q, zero), jnp.where(lane >= DIFF_HEAD_DIM, q, zero)]
    n_sub = tq // LANES
    n_full = i * (tq // tk)

    def load_k(j, hh):
        k0 = pl.multiple_of(j * tk, tk)
        return k0, k_ref[pl.ds(k0, tk), hh * LANES:(hh + 1) * LANES]

    def load_v(j, hh):
        return vt_ref[hh * LANES:(hh + 1) * LANES, pl.ds(pl.multiple_of(j * tk, tk), tk)]

    def scores(kb, u, c, k0, masked):
        s = lax.dot_general(kb, qs[u][c * LANES:(c + 1) * LANES, :], (((1,), (1,)), ((), ())),
                            preferred_element_type=F32)
        if masked:
            kc = (k0 + lax.broadcasted_iota(jnp.int32, (tk, LANES), 0)) // CHUNK
            qc = (i * tq + c * LANES + lax.broadcasted_iota(jnp.int32, (tk, LANES), 1)) // CHUNK
            s = jnp.where(kc <= qc, s, -1e30)
        return s

    def finish(ls):
        lam = lam_ref[0:1, 0:1]
        for hh in range(heads):
            o = acc_scr[2 * hh] / ls[2 * hh] - lam * (acc_scr[2 * hh + 1] / ls[2 * hh + 1])
            o = o * lax.rsqrt(jnp.mean(o * o, axis=0, keepdims=True) + EPS) * sw_ref[...] * out_scale
            o_ref[hh * LANES:(hh + 1) * LANES, :] = o.astype(o_ref.dtype)

    acc_scr[...] = jnp.zeros_like(acc_scr)

    @pl.when(bound_ref[0] == 1)
    def _():
        l_scr[...] = jnp.zeros_like(l_scr)
        pp_scr[1] = jnp.zeros(pp_scr.shape[1:], BF16)

        def values(j, hh):
            vb = load_v(j, hh)
            for u in (2 * hh, 2 * hh + 1):
                acc_scr[u] = acc_scr[u] + jnp.dot(vb, pp_scr[j % 2, u], preferred_element_type=F32)

        def visibility(jj, c):
            k_lo, k_hi = (jj * tk) // CHUNK, ((jj + 1) * tk - 1) // CHUNK
            q_lo, q_hi = (c * LANES) // CHUNK, ((c + 1) * LANES - 1) // CHUNK
            return "full" if k_hi <= q_lo else ("none" if k_lo > q_hi else "part")

        def step(j, jj):
            jp = jnp.maximum(j - 1, 0) + (j == 0).astype(jnp.int32)
            for hh in range(heads):
                values(jp, hh)
                k0, kb = load_k(j, hh)
                for u in (2 * hh, 2 * hh + 1):
                    for c in range(n_sub):
                        cs = slice(c * LANES, (c + 1) * LANES)
                        vis = "full" if jj is None else visibility(jj, c)
                        if vis == "none":
                            pp_scr[j % 2, u, :, cs] = jnp.zeros((tk, LANES), BF16)
                            continue
                        p = jnp.exp2(scores(kb, u, c, k0, vis == "part"))
                        l_scr[u, :, cs] = l_scr[u, :, cs] + jnp.sum(p.reshape(tk // 8, 8, LANES), axis=0)
                        pp_scr[j % 2, u, :, cs] = p.astype(BF16)

        def body(j, carry):
            step(j, None)
            return carry

        lax.fori_loop(0, n_full, body, 0)
        n_mask = tq // tk
        for jj in range(n_mask):
            step(n_full + jj, jj)
        for hh in range(heads):
            values(n_full + n_mask - 1, hh)
        finish([jnp.sum(l_scr[u], axis=0, keepdims=True) for u in range(2 * heads)])

    @pl.when(bound_ref[0] != 1)
    def _():
        neg = jnp.full((1, tq), -1e30, F32)
        zer = jnp.zeros((1, tq), F32)

        def block(j, carry, masked):
            out = []
            for hh in range(heads):
                k0, kb = load_k(j, hh)
                vb = load_v(j, hh)
                alphas = []
                for u in (2 * hh, 2 * hh + 1):
                    m, l = carry[2 * u], carry[2 * u + 1]
                    m_parts, l_parts, a_parts = [], [], []
                    for c in range(n_sub):
                        cs = slice(c * LANES, (c + 1) * LANES)
                        s = scores(kb, u, c, k0, masked)
                        m_new = jnp.maximum(m[:, cs], jnp.max(s, axis=0, keepdims=True))
                        p = jnp.exp2(s - m_new)
                        alpha = jnp.exp2(m[:, cs] - m_new)
                        l_parts.append(alpha * l[:, cs] + jnp.sum(p, axis=0, keepdims=True))
                        pp_scr[0, u, :, cs] = p.astype(BF16)
                        m_parts.append(m_new)
                        a_parts.append(alpha)
                    out += [jnp.concatenate(m_parts, axis=1), jnp.concatenate(l_parts, axis=1)]
                    alphas.append(jnp.concatenate(a_parts, axis=1))
                for t, u in enumerate((2 * hh, 2 * hh + 1)):
                    acc_scr[u] = alphas[t] * acc_scr[u] + jnp.dot(vb, pp_scr[0, u], preferred_element_type=F32)
            return tuple(out)

        carry = lax.fori_loop(0, n_full, lambda j, c: block(j, c, False), (neg, zer) * (2 * heads))
        for jj in range(tq // tk):
            carry = block(n_full + jj, carry, True)
        finish([carry[2 * u + 1] for u in range(2 * heads)])


ATTN_HEADS_PER_STEP = 4
SCORE_BOUND_LOG2 = 64.0


def _attention(q, k, vt, score_bound, lam_vec, subln_col, batch, seq, out_scale):
    tq = _row_tile(seq, 512)
    tk = _row_tile(tq, 256)
    nq = seq // tq
    assert seq // tk >= 2
    hp = ATTN_HEADS_PER_STEP
    wid = hp * LANES
    kern = functools.partial(_attn_kernel, tq=tq, tk=tk, heads=hp, out_scale=out_scale)
    full = lambda a: pl.BlockSpec(a.shape, lambda b, h, i: (0,) * a.ndim)
    bounded = (score_bound <= SCORE_BOUND_LOG2).astype(jnp.int32).reshape(1)
    return pl.pallas_call(
        kern,
        grid=(batch, DIFF_HEADS // hp, nq),
        in_specs=[pl.BlockSpec(memory_space=pltpu.SMEM),
                  pl.BlockSpec((tq, wid), lambda b, h, i: (b * nq + i, h)),
                  pl.BlockSpec((seq, wid), lambda b, h, i: (b, h)),
                  pl.BlockSpec((wid, seq), lambda b, h, i: (h, b)),
                  full(lam_vec), full(subln_col)],
        out_specs=pl.BlockSpec((wid, tq), lambda b, h, i: (h, b * nq + i)),
        out_shape=jax.ShapeDtypeStruct(vt.shape, BF16),
        scratch_shapes=[pltpu.VMEM((2 * hp, LANES, tq), F32),
                        pltpu.VMEM((2, 2 * hp, tk, tq), BF16), pltpu.VMEM((2 * hp, 8, tq), F32)],
        compiler_params=_cparams("parallel", "parallel", "arbitrary"),
        name="diff_attn",
    )(bounded, q, k, vt, lam_vec, subln_col)


def _router_kernel(x_ref, nw_ref, sc_ref, sh_ref, whi_ref, wlo_ref, h_ref, idx_ref, gate_ref):
    h = _norm_mod(x_ref[...], nw_ref[...], sc_ref[0], sh_ref[0])
    h_ref[...] = h
    h_hi = h.astype(BF16)
    h_lo = (h - h_hi.astype(F32)).astype(BF16)
    logits = (jnp.dot(h_hi, whi_ref[...], preferred_element_type=F32)
              + jnp.dot(h_hi, wlo_ref[...], preferred_element_type=F32)
              + jnp.dot(h_lo, whi_ref[...], preferred_element_type=F32))
    lane = lax.broadcasted_iota(jnp.int32, logits.shape, 1)
    lane_f = lane.astype(F32)
    logits = jnp.where(lane < N_EXPERTS, logits, -jnp.inf)
    m1 = jnp.max(logits, axis=-1, keepdims=True)
    i1 = jnp.min(jnp.where(logits == m1, lane_f, float(LANES)), axis=-1, keepdims=True)
    rest = jnp.where(lane_f == i1, -jnp.inf, logits)
    m2 = jnp.max(rest, axis=-1, keepdims=True)
    i2 = jnp.min(jnp.where(rest == m2, lane_f, float(LANES)), axis=-1, keepdims=True)
    e = jnp.exp(m2 - m1)
    g1 = 1.0 / (1.0 + e)
    idx_ref[...] = jnp.where(lane == 0, i1, jnp.where(lane == 1, i2, 0.0)).astype(jnp.int32)
    gate_ref[...] = jnp.where(lane == 0, g1, jnp.where(lane == 1, e * g1, 0.0))


def _router(x, seq, nw, sc, sh, w_hi, w_lo):
    t_rows = x.shape[0]
    tm = _row_tile(seq, 512)
    row = lambda w: pl.BlockSpec((tm, w), lambda i: (i, 0))
    full = lambda a: pl.BlockSpec(a.shape, lambda i: (0,) * a.ndim)
    return pl.pallas_call(
        _router_kernel,
        grid=(t_rows // tm,),
        in_specs=[row(D_MODEL), full(nw), _mod_spec(tm, seq), _mod_spec(tm, seq), full(w_hi), full(w_lo)],
        out_specs=[row(D_MODEL), row(LANES), row(LANES)],
        out_shape=[jax.ShapeDtypeStruct((t_rows, D_MODEL), F32),
                   jax.ShapeDtypeStruct((t_rows, LANES), jnp.int32),
                   jax.ShapeDtypeStruct((t_rows, LANES), F32)],
        compiler_params=_cparams("parallel"),
        name="router",
    )(x, nw, sc, sh, w_hi, w_lo)


def _start_row_gather(idx_ref, src_hbm, dst_ref, sem, n_rows):
    def start(r2, _):
        for prio in range(2):
            r = 2 * r2 + prio
            pltpu.make_async_copy(src_hbm.at[pl.ds(idx_ref[0, 0, r], 1), :], dst_ref.at[pl.ds(r, 1), :],
                                  sem).start(priority=prio)
        return 0

    lax.fori_loop(0, n_rows // 2, start, 0, unroll=4)


def _wait_row_gather(src_hbm, dst_ref, sem, n_rows):
    pltpu.make_async_copy(src_hbm.at[pl.ds(0, n_rows), :], dst_ref, sem).wait()


def _prefetched_rows(i, idx_cur_ref, idx_next_ref, src_hbm, buf, sems, n_rows):
    slot = i % 2

    @pl.when(i == 0)
    def _():
        _start_row_gather(idx_cur_ref, src_hbm, buf.at[0], sems.at[0], n_rows)

    @pl.when(i + 1 < pl.num_programs(0))
    def _():
        _start_row_gather(idx_next_ref, src_hbm, buf.at[1 - slot], sems.at[1 - slot], n_rows)

    _wait_row_gather(src_hbm, buf.at[slot], sems.at[slot], n_rows)
    return buf.at[slot]


def _expert_kernel(te_ref, nt_ref, idx_cur_ref, idx_next_ref, h_hbm, w1_ref, w3_ref, w2_ref, o_ref, xbuf, sems):
    i = pl.program_id(0)
    x_ref = _prefetched_rows(i, idx_cur_ref, idx_next_ref, h_hbm, xbuf, sems, EXPERT_TM)

    @pl.when(i < nt_ref[0])
    def _():
        h = x_ref[...].astype(BF16)
        a = jnp.dot(h, w1_ref[0], preferred_element_type=F32)
        b = jnp.dot(h, w3_ref[0], preferred_element_type=F32)
        o_ref[...] = jnp.dot((_silu(a) * b).astype(BF16), w2_ref[0], preferred_element_type=F32)

    @pl.when(i >= nt_ref[0])
    def _():
        o_ref[...] = jnp.zeros_like(o_ref)


EXPERT_TM = 256


def _expert_ffn(h, tok_of_slot, tile_expert, n_tiles_used, w13, w2):
    tm = EXPERT_TM
    n_tiles = tok_of_slot.shape[0] // tm
    idx = tok_of_slot.reshape(n_tiles, 1, tm)
    return pl.pallas_call(
        _expert_kernel,
        grid_spec=pltpu.PrefetchScalarGridSpec(
            num_scalar_prefetch=2,
            grid=(n_tiles,),
            in_specs=[pl.BlockSpec((1, 1, tm), lambda i, te, nt: (i, 0, 0), memory_space=pltpu.SMEM),
                      pl.BlockSpec((1, 1, tm), lambda i, te, nt: (jnp.minimum(i + 1, n_tiles - 1), 0, 0),
                                   memory_space=pltpu.SMEM),
                      pl.BlockSpec(memory_space=pl.ANY),
                      pl.BlockSpec((1, D_MODEL, D_FF), lambda i, te, nt: (te[i], 0, 0)),
                      pl.BlockSpec((1, D_MODEL, D_FF), lambda i, te, nt: (te[i], 0, 1)),
                      pl.BlockSpec((1, D_FF, D_MODEL), lambda i, te, nt: (te[i], 0, 0))],
            out_specs=pl.BlockSpec((tm, D_MODEL), lambda i, te, nt: (i, 0)),
            scratch_shapes=[pltpu.VMEM((2, tm, D_MODEL), F32), pltpu.SemaphoreType.DMA((2,))],
        ),
        out_shape=jax.ShapeDtypeStruct((n_tiles * tm, D_MODEL), F32),
        compiler_params=_cparams("arbitrary"),
        name="expert_ffn",
    )(tile_expert, n_tiles_used, idx, idx, h, w13, w13, w2)


COMBINE_TM = 256


def _combine_kernel(p0_cur, p0_next, p1_cur, p1_next, x_ref, g_ref, gate_ref, y_hbm, o_ref, buf0, buf1, sems0,
                    sems1):
    i = pl.program_id(0)
    y0 = _prefetched_rows(i, p0_cur, p0_next, y_hbm, buf0, sems0, COMBINE_TM)
    y1 = _prefetched_rows(i, p1_cur, p1_next, y_hbm, buf1, sems1, COMBINE_TM)
    gates = gate_ref[...]
    f = gates[:, 0:1] * y0[...] + gates[:, 1:2] * y1[...]
    o_ref[...] = x_ref[...] + g_ref[0] * f


def _combine(x, gate_mod, seq, gates, y_perm, pos):
    t_rows = x.shape[0]
    tm = COMBINE_TM
    nt = t_rows // tm
    row = pl.BlockSpec((tm, D_MODEL), lambda i: (i, 0))
    cur = pl.BlockSpec((1, 1, tm), lambda i: (i, 0, 0), memory_space=pltpu.SMEM)
    nxt = pl.BlockSpec((1, 1, tm), lambda i: (jnp.minimum(i + 1, nt - 1), 0, 0), memory_space=pltpu.SMEM)
    p0 = pos[:, 0].reshape(nt, 1, tm)
    p1 = pos[:, 1].reshape(nt, 1, tm)
    return pl.pallas_call(
        _combine_kernel,
        grid=(nt,),
        in_specs=[cur, nxt, cur, nxt, row, _mod_spec(tm, seq), pl.BlockSpec((tm, LANES), lambda i: (i, 0)),
                  pl.BlockSpec(memory_space=pl.ANY)],
        out_specs=row,
        out_shape=jax.ShapeDtypeStruct(x.shape, F32),
        scratch_shapes=[pltpu.VMEM((2, tm, D_MODEL), F32), pltpu.VMEM((2, tm, D_MODEL), F32),
                        pltpu.SemaphoreType.DMA((2,)), pltpu.SemaphoreType.DMA((2,))],
        compiler_params=_cparams("arbitrary"),
        name="moe_combine",
    )(p0, p0, p1, p1, x, gate_mod, gates, y_perm)


def _dispatch_plan(expert_idx, t_rows):
    tm = EXPERT_TM
    flat = expert_idx.reshape(-1)
    onehot = (flat[:, None] == jnp.arange(N_EXPERTS)[None, :]).astype(jnp.int32)
    rank = jnp.take_along_axis(jnp.cumsum(onehot, axis=0) - onehot, flat[:, None], axis=1)[:, 0]
    counts = jnp.sum(onehot, axis=0)
    tiles_per = (counts + tm - 1) // tm
    tile_end = jnp.cumsum(tiles_per)
    offs = (tile_end - tiles_per) * tm
    pos = offs[flat] + rank
    n_tiles = (2 * t_rows) // tm + N_EXPERTS
    p_rows = n_tiles * tm
    tok_of_slot = jnp.zeros((p_rows,), jnp.int32).at[pos].set(jnp.arange(2 * t_rows, dtype=jnp.int32) // 2)
    tile_expert = jnp.minimum(
        jnp.sum(jnp.arange(n_tiles, dtype=jnp.int32)[:, None] >= tile_end[None, :], axis=1), N_EXPERTS - 1
    ).astype(jnp.int32)
    n_used = tile_end[-1:].astype(jnp.int32)
    pos2 = pos.reshape(t_rows, 2)
    return tok_of_slot, tile_expert, n_used, pos2.astype(jnp.int32)


def _mods(mod, layer, batch):
    m = mod[layer, :batch].reshape(batch, 6, 1, D_MODEL)
    return [m[:, j] for j in range(6)]


def _even_layer(x, batch, seq, mods, nw_mix, nw_ffn, w_in, conv_w, a_log, dt_bias, gdn_norm_w,
                lam_re, lam_im, log_step, b_re, b_im, c_re, c_im, d_skip, glu_w, glu_b, w_out, w13, w2):
    sh1, sc1, g1, sh2, sc2, g2 = mods
    t_rows = batch * seq
    qkvz, rest = w_in[:, :4 * GDN_WIDTH], w_in[:, 4 * GDN_WIDTH:]
    w_ba, w_u = rest[:, :2 * GDN_HEADS], rest[:, 2 * GDN_HEADS:]
    w_cat = jnp.concatenate(
        [qkvz, w_u, w_ba, jnp.zeros((D_MODEL, LANES - 2 * GDN_HEADS), w_in.dtype)], axis=1).astype(BF16)
    gate_params = jnp.zeros((8, LANES), F32)
    gate_params = gate_params.at[0, GDN_HEADS:2 * GDN_HEADS].set(a_log.astype(F32))
    gate_params = gate_params.at[1, GDN_HEADS:2 * GDN_HEADS].set(dt_bias.astype(F32))
    q, k, v, z, u, gb = _even_in(x, seq, nw_mix, sc1, sh1, w_cat, conv_w.astype(F32), gate_params)
    y_a = _gdn(q, k, v, z, gb, gdn_norm_w.reshape(1, GDN_HEAD_DIM).astype(F32), batch, seq)

    seg = seq // S5_Q
    n_lev = max(int(math.ceil(math.log2(seg))), 0)
    toep, e_mat, f_mat, a1, a2 = _s5_tables(lam_re, lam_im, log_step, b_re, b_im, c_re, c_im, max(n_lev, 1))
    m_rows = t_rows // S5_Q
    u_grp = (u.reshape(m_rows, S5_Q, S5_GROUPS, S5_GROUP).transpose(2, 0, 1, 3)
             .reshape(S5_GROUPS, m_rows, S5_Q * S5_GROUP).astype(BF16))
    yg = _s5_conv(u_grp, toep, e_mat, f_mat, a1, a2, seg, n_lev)
    yconv = (yg.reshape(S5_GROUPS, m_rows, S5_Q, S5_GROUP).transpose(1, 2, 0, 3)
             .reshape(t_rows, S5_WIDTH))
    y_b = _s5_glu(yconv, u, d_skip.reshape(1, S5_WIDTH).astype(F32), glu_w.astype(BF16),
                  glu_b.reshape(1, 2 * S5_WIDTH).astype(F32))
    wo = w_out.astype(BF16)
    x = _proj_res(x, g1, seq, [y_a, y_b], [wo[:GDN_WIDTH], wo[GDN_WIDTH:]])
    return _ffn(x, seq, nw_ffn, sc2, sh2, g2, w13.astype(BF16), w2.astype(BF16))


def _odd_layer(x, batch, seq, mods, nw_mix, nw_ffn, w_qkv, q_norm_w, k_norm_w, lq1, lk1, lq2, lk2,
               subln_w, w_out, router_w, w13, w2, lambda_init):
    sh1, sc1, g1, sh2, sc2, g2 = mods
    t_rows = batch * seq
    tile2 = lambda w: jnp.tile(w.astype(F32), LANES // DIFF_HEAD_DIM).reshape(1, LANES)
    w_qkv_b = w_qkv.astype(BF16)
    q, k, vt = _odd_qkv(x, seq, nw_mix, sc1, sh1, w_qkv_b[:, :2 * D_MODEL], w_qkv_b[:, 2 * D_MODEL:].T,
                        tile2(q_norm_w), tile2(k_norm_w))
    lam = (jnp.exp(jnp.sum(lq1.astype(F32) * lk1.astype(F32)))
           - jnp.exp(jnp.sum(lq2.astype(F32) * lk2.astype(F32))) + lambda_init)
    lam_vec = jnp.full((8, LANES), lam, F32)
    score_bound = (1.02 * LOG2E * DIFF_HEAD_DIM ** 0.5
                   * jnp.max(jnp.abs(q_norm_w.astype(F32))) * jnp.max(jnp.abs(k_norm_w.astype(F32))))
    ot = _attention(q, k, vt, score_bound, lam_vec, subln_w.reshape(LANES, 1).astype(F32), batch, seq,
                    1.0 - lambda_init)
    x = _proj_res_t(x, g1, seq, ot, w_out.astype(BF16))

    rw = jnp.zeros((D_MODEL, LANES), F32).at[:, :N_EXPERTS].set(router_w.astype(F32))
    rw_hi = rw.astype(BF16)
    rw_lo = (rw - rw_hi.astype(F32)).astype(BF16)
    h, idx, gates = _router(x, seq, nw_ffn, sc2, sh2, rw_hi, rw_lo)
    tok_of_slot, tile_expert, n_used, pos = _dispatch_plan(idx[:, :2], t_rows)
    y_perm = _expert_ffn(h, tok_of_slot, tile_expert, n_used, w13.astype(BF16), w2.astype(BF16))
    return _combine(x, g2, seq, gates, y_perm, pos)


def kernel(x, c, ada_w, ada_b, norm_mix_w, norm_ffn_w, even_w_in, even_conv_w, even_a_log, even_dt_bias, even_gdn_norm_w, even_lam_re, even_lam_im, even_log_step, even_b_re, even_b_im, even_c_re, even_c_im, even_d_skip, even_glu_w, even_glu_b, even_w_out, even_ffn_w13, even_ffn_w2, odd_w_qkv, odd_q_norm_w, odd_k_norm_w, odd_lambda_q1, odd_lambda_k1, odd_lambda_q2, odd_lambda_k2, odd_subln_w, odd_w_out, odd_router_w, odd_expert_w13, odd_expert_w2):
    batch, seq, d = x.shape
    assert d == D_MODEL and seq % CHUNK == 0 and seq % S5_Q == 0
    c_pad = jnp.zeros((8, d), F32).at[:batch].set(c.astype(F32))
    mod = _adaln(c_pad, ada_w.astype(BF16), ada_b.astype(F32))
    xf = x.astype(F32).reshape(batch * seq, d)
    for layer in range(DEPTH):
        i = layer // 2
        mods = _mods(mod, layer, batch)
        nw_mix = norm_mix_w[layer].reshape(1, d).astype(F32)
        nw_ffn = norm_ffn_w[layer].reshape(1, d).astype(F32)
        if layer % 2 == 0:
            xf = _even_layer(xf, batch, seq, mods, nw_mix, nw_ffn, even_w_in[i], even_conv_w[i], even_a_log[i],
                             even_dt_bias[i], even_gdn_norm_w[i], even_lam_re[i], even_lam_im[i],
                             even_log_step[i], even_b_re[i], even_b_im[i], even_c_re[i], even_c_im[i],
                             even_d_skip[i], even_glu_w[i], even_glu_b[i], even_w_out[i],
                             even_ffn_w13[i], even_ffn_w2[i])
        else:
            lambda_init = 0.8 - 0.6 * math.exp(-0.3 * layer)
            xf = _odd_layer(xf, batch, seq, mods, nw_mix, nw_ffn, odd_w_qkv[i], odd_q_norm_w[i], odd_k_norm_w[i],
                            odd_lambda_q1[i], odd_lambda_k1[i], odd_lambda_q2[i], odd_lambda_k2[i],
                            odd_subln_w[i], odd_w_out[i], odd_router_w[i], odd_expert_w13[i],
                            odd_expert_w2[i], lambda_init)
    return xf.reshape(batch, seq, d).astype(x.dtype)
```

```python
import functools
import math

import jax
import jax.numpy as jnp
from jax import lax
from jax.experimental import pallas as pl
from jax.experimental.pallas import tpu as pltpu

F32 = jnp.float32
BF16 = jnp.bfloat16

D_MODEL = 1024
DEPTH = 4
CHUNK = 64
EPS = 1e-6
GDN_HEAD_DIM = 128
GDN_WIDTH = D_MODEL // 2
GDN_HEADS = GDN_WIDTH // GDN_HEAD_DIM
CONV_K = 4
QKV_W = 3 * GDN_WIDTH
S5_WIDTH = D_MODEL - GDN_WIDTH
S5_GROUP = 16
S5_GROUPS = S5_WIDTH // S5_GROUP
S5_STATE = 64
S5_Q = 16
DIFF_HEAD_DIM = 64
DIFF_HEADS = D_MODEL // (2 * DIFF_HEAD_DIM)
D_FF = ((8 * D_MODEL // 3 + 127) // 128) * 128
N_EXPERTS = 8
LANES = 128
EVEN_IN_COLS = QKV_W + GDN_WIDTH + S5_WIDTH + LANES
VMEM_LIMIT = 56 * 1024 * 1024


def _cparams(*sem):
    return pltpu.CompilerParams(dimension_semantics=sem, vmem_limit_bytes=VMEM_LIMIT)


def _mm(a, b):
    return jnp.dot(a.astype(BF16), b.astype(BF16), preferred_element_type=F32)


def _mm_nt(a, b):
    return lax.dot_general(a.astype(BF16), b.astype(BF16), (((1,), (1,)), ((), ())),
                           preferred_element_type=F32)


def _mm_tn(a, b):
    return lax.dot_general(a.astype(BF16), b.astype(BF16), (((0,), (0,)), ((), ())),
                           preferred_element_type=F32)


def _mm_hi(a, b):
    a_hi = a.astype(BF16)
    a_lo = (a - a_hi.astype(F32)).astype(BF16)
    b_hi = b.astype(BF16)
    b_lo = (b - b_hi.astype(F32)).astype(BF16)
    return (jnp.dot(a_hi, b_hi, preferred_element_type=F32)
            + (jnp.dot(a_hi, b_lo, preferred_element_type=F32) + jnp.dot(a_lo, b_hi, preferred_element_type=F32)))


def _split3(x):
    hi = x.astype(BF16)
    r = x - hi.astype(F32)
    mid = r.astype(BF16)
    lo = (r - mid.astype(F32)).astype(BF16)
    return hi, mid, lo


def _norm_mod(x, nw, sc, sh):
    ms = jnp.mean(x * x, axis=-1, keepdims=True)
    return (x * lax.rsqrt(ms + EPS) * nw) * (1.0 + sc) + sh


def _silu(x):
    return x * jax.nn.sigmoid(x)


def _row_tile(n, pref):
    t = min(pref, n)
    while n % t:
        t //= 2
    return t


def _adaln_kernel(c_ref, w_ref, b_ref, o_ref):
    c = c_ref[...]
    o_ref[0] = _mm(_silu(c), w_ref[0]) + b_ref[0]


def _adaln(c_pad, ada_w, ada_b):
    depth, d, n = ada_w.shape
    tn = 1536
    rows = c_pad.shape[0]
    return pl.pallas_call(
        _adaln_kernel,
        grid=(depth, n // tn),
        in_specs=[pl.BlockSpec((rows, d), lambda l, j: (0, 0)),
                  pl.BlockSpec((1, d, tn), lambda l, j: (l, 0, j)),
                  pl.BlockSpec((1, 1, tn), lambda l, j: (l, 0, j))],
        out_specs=pl.BlockSpec((1, rows, tn), lambda l, j: (l, 0, j)),
        out_shape=jax.ShapeDtypeStruct((depth, rows, n), F32),
        compiler_params=_cparams("parallel", "parallel"),
        name="adaln",
    )(c_pad, ada_w, ada_b.reshape(depth, 1, n))


def _mod_spec(tm, seq):
    return pl.BlockSpec((1, 1, D_MODEL), lambda i, *_: ((i * tm) // seq, 0, 0))


def _even_in_kernel(x_ref, xh_ref, nw_ref, sc_ref, sh_ref, w_ref, cw_ref, gp_ref,
                    q_ref, k_ref, v_ref, z_ref, u_ref, gb_ref, pre_scr, *, tm, seq):
    i = pl.program_id(0)
    nw = nw_ref[...]
    sc = sc_ref[0]
    sh = sh_ref[0]
    h = _norm_mod(x_ref[...], nw, sc, sh).astype(BF16)
    proj = jnp.dot(h, w_ref[...], preferred_element_type=F32)
    hh = _norm_mod(xh_ref[...], nw, sc, sh).astype(BF16)
    preh = jnp.dot(hh, w_ref[:, :QKV_W], preferred_element_type=F32)
    preh = jnp.where((i * tm) % seq == 0, 0.0, preh)
    pre = proj[:, :QKV_W]
    pre_scr[0:8, :] = preh
    pre_scr[8:8 + tm, :] = pre
    acc = pre * cw_ref[CONV_K - 1:CONV_K, :]
    for j in range(CONV_K - 1):
        acc = acc + pre_scr[pl.ds(8 - (CONV_K - 1) + j, tm), :] * cw_ref[j:j + 1, :]
    qkv = _silu(acc)
    for hd in range(GDN_HEADS):
        lo = hd * GDN_HEAD_DIM
        qh = qkv[:, lo:lo + GDN_HEAD_DIM]
        kh = qkv[:, GDN_WIDTH + lo:GDN_WIDTH + lo + GDN_HEAD_DIM]
        q_ref[:, lo:lo + GDN_HEAD_DIM] = (
            qh * lax.rsqrt(jnp.sum(qh * qh, axis=-1, keepdims=True) + EPS) * GDN_HEAD_DIM ** -0.5)
        k_ref[:, lo:lo + GDN_HEAD_DIM] = kh * lax.rsqrt(jnp.sum(kh * kh, axis=-1, keepdims=True) + EPS)
    v_ref[...] = qkv[:, 2 * GDN_WIDTH:]
    z_ref[...] = proj[:, QKV_W:QKV_W + GDN_WIDTH]
    u_ref[...] = proj[:, QKV_W + GDN_WIDTH:QKV_W + GDN_WIDTH + S5_WIDTH]
    ba = proj[:, QKV_W + GDN_WIDTH + S5_WIDTH:]
    lane = lax.broadcasted_iota(jnp.int32, ba.shape, 1)
    t = ba + gp_ref[1:2, :]
    softplus = jnp.maximum(t, 0.0) + jnp.log1p(jnp.exp(-jnp.abs(t)))
    g = -jnp.exp(gp_ref[0:1, :]) * softplus
    gb_ref[...] = jnp.where(lane < GDN_HEADS, jax.nn.sigmoid(ba), g)


def _even_in(x, seq, nw, sc, sh, w_cat, conv_w, gate_params):
    t_rows = x.shape[0]
    tm = _row_tile(seq, 512)
    kern = functools.partial(_even_in_kernel, tm=tm, seq=seq)
    row = lambda w: pl.BlockSpec((tm, w), lambda i: (i, 0))
    full = lambda a: pl.BlockSpec(a.shape, lambda i: (0,) * a.ndim)
    outs = [jax.ShapeDtypeStruct((t_rows, GDN_WIDTH), F32)] * 5 + [jax.ShapeDtypeStruct((t_rows, LANES), F32)]
    return pl.pallas_call(
        kern,
        grid=(t_rows // tm,),
        in_specs=[row(D_MODEL),
                  pl.BlockSpec((8, D_MODEL), lambda i: (jnp.maximum(i * (tm // 8) - 1, 0), 0)),
                  full(nw), _mod_spec(tm, seq), _mod_spec(tm, seq), full(w_cat), full(conv_w),
                  full(gate_params)],
        out_specs=[row(GDN_WIDTH)] * 5 + [row(LANES)],
        out_shape=outs,
        scratch_shapes=[pltpu.VMEM((tm + 8, QKV_W), F32)],
        compiler_params=_cparams("parallel"),
        name="even_in",
    )(x, x, nw, sc, sh, w_cat, conv_w, gate_params)


def _gdn_kernel(q_ref, k_ref, v_ref, z_ref, gb_ref, nw_ref, ya_ref, state_scr, *, batch):
    @pl.when(pl.program_id(0) == 0)
    def _():
        state_scr[...] = jnp.zeros_like(state_scr)

    r64 = lax.broadcasted_iota(jnp.int32, (CHUNK, CHUNK), 0)
    c64 = lax.broadcasted_iota(jnp.int32, (CHUNK, CHUNK), 1)
    causal = r64 >= c64
    strict = r64 > c64
    tri = causal.astype(BF16)
    sel = (lax.broadcasted_iota(jnp.int32, (8, LANES), 1)
           == lax.broadcasted_iota(jnp.int32, (8, LANES), 0) + GDN_HEADS).astype(BF16)
    nw = nw_ref[...]
    units = [(b, hd) for b in range(batch) for hd in range(GDN_HEADS)]
    gbs = [gb_ref[b] for b in range(batch)]
    gcs = [sum(jnp.dot(tri, p, preferred_element_type=F32) for p in _split3(gb)) for gb in gbs]
    gcts = [sum(lax.dot_general(sel, p, (((1,), (1,)), ((), ())), preferred_element_type=F32)
                for p in _split3(gc)) for gc in gcs]

    qs, ks, gcols, egs, kbs = [], [], [], [], []
    for b, hd in units:
        sl = slice(hd * GDN_HEAD_DIM, (hd + 1) * GDN_HEAD_DIM)
        qs.append(q_ref[b, :, sl])
        ks.append(k_ref[b, :, sl])
        gcols.append(gcs[b][:, GDN_HEADS + hd:GDN_HEADS + hd + 1])
        egs.append(jnp.exp(gcols[-1]))
        kbs.append(ks[-1] * gbs[b][:, hd:hd + 1])
    kks = [_mm_nt(kb, kh) for kb, kh in zip(kbs, ks)]
    qks = [_mm_nt(qh, kh) for qh, kh in zip(qs, ks)]
    p_mats, sols = [], []
    for u, (b, hd) in enumerate(units):
        sl = slice(hd * GDN_HEAD_DIM, (hd + 1) * GDN_HEAD_DIM)
        decay = jnp.exp(jnp.where(causal, gcols[u] - gcts[b][hd:hd + 1, :], -1e30))
        p_mats.append(jnp.where(strict, -kks[u] * decay, 0.0))
        qks[u] = jnp.where(causal, qks[u] * decay, 0.0)
        sols.append(jnp.concatenate([v_ref[b, :, sl] * gbs[b][:, hd:hd + 1], kbs[u] * egs[u]], axis=1))
    n_fac = int(math.log2(CHUNK))
    for it in range(n_fac):
        sols = [sol + _mm_hi(p, sol) for p, sol in zip(p_mats, sols)]
        if it + 1 < n_fac:
            p_mats = [_mm_hi(p, p) for p in p_mats]
    sts = [state_scr[u] for u in range(len(units))]
    ws = [_mm(sol[:, GDN_HEAD_DIM:], st) for sol, st in zip(sols, sts)]
    os_ = [_mm(qh * eg, st) for qh, eg, st in zip(qs, egs, sts)]
    v_news = [sol[:, :GDN_HEAD_DIM] - w for sol, w in zip(sols, ws)]
    os_ = [o + _mm(qk, vn) for o, qk, vn in zip(os_, qks, v_news)]
    for u, (b, hd) in enumerate(units):
        sl = slice(hd * GDN_HEAD_DIM, (hd + 1) * GDN_HEAD_DIM)
        glast = gcols[u][CHUNK - 1:CHUNK, :]
        kd = ks[u] * jnp.exp(glast - gcols[u])
        state_scr[u] = sts[u] * jnp.exp(glast) + _mm_tn(kd, v_news[u])
        o = os_[u]
        on = o * lax.rsqrt(jnp.mean(o * o, axis=-1, keepdims=True) + EPS) * nw
        ya_ref[b, :, sl] = (on * _silu(z_ref[b, :, sl])).astype(ya_ref.dtype)


def _gdn(q, k, v, z, gb, norm_w, batch, seq):
    nc = seq // CHUNK
    t_rows = q.shape[0]
    blk = lambda w: pl.BlockSpec((batch, CHUNK, w), lambda c: (0, c, 0))
    r3 = lambda a: a.reshape(batch, seq, a.shape[-1])
    out = pl.pallas_call(
        functools.partial(_gdn_kernel, batch=batch),
        grid=(nc,),
        in_specs=[blk(GDN_WIDTH)] * 4 + [blk(LANES), pl.BlockSpec((1, GDN_HEAD_DIM), lambda c: (0, 0))],
        out_specs=blk(GDN_WIDTH),
        out_shape=jax.ShapeDtypeStruct((batch, seq, GDN_WIDTH), BF16),
        scratch_shapes=[pltpu.VMEM((batch * GDN_HEADS, GDN_HEAD_DIM, GDN_HEAD_DIM), F32)],
        compiler_params=_cparams("arbitrary"),
        name="gdn",
    )(r3(q), r3(k), r3(v), r3(z), r3(gb), norm_w)
    return out.reshape(t_rows, GDN_WIDTH)


def _s5_tables(lam_re, lam_im, log_step, b_re, b_im, c_re, c_im, n_lev):
    hp = lax.Precision.HIGHEST
    lr = jnp.minimum(lam_re.astype(F32), -1e-4)
    li = lam_im.astype(F32)
    dt = jnp.exp(log_step.astype(F32))[:, None]
    js = jnp.arange(S5_Q + 1, dtype=F32)[:, None, None]
    mag = jnp.exp(lr * dt * js)
    pr, pi = mag * jnp.cos(li * dt * js), mag * jnp.sin(li * dt * js)
    ar, ai = pr[1], pi[1]
    nr, ni = ar - 1.0, ai
    den = lr * lr + li * li
    cr, ci = (nr * lr + ni * li) / den, (ni * lr - nr * li) / den
    b_re, b_im = b_re.astype(F32), b_im.astype(F32)
    bbr = cr[..., None] * b_re - ci[..., None] * b_im
    bbi = cr[..., None] * b_im + ci[..., None] * b_re
    c_re, c_im = c_re.astype(F32), c_im.astype(F32)
    lbr = pr[..., None] * bbr - pi[..., None] * bbi
    lbi = pr[..., None] * bbi + pi[..., None] * bbr
    kern = (jnp.einsum('gon,jgni->gjoi', c_re, lbr[:S5_Q], precision=hp)
            - jnp.einsum('gon,jgni->gjoi', c_im, lbi[:S5_Q], precision=hp))
    s_idx = jnp.arange(S5_Q)[:, None]
    t_idx = jnp.arange(S5_Q)[None, :]
    lag = jnp.clip(t_idx - s_idx, 0, S5_Q - 1)
    toep = jnp.where((t_idx >= s_idx)[None, :, :, None, None], kern[:, lag], 0.0)
    toep = toep.transpose(0, 1, 4, 2, 3).reshape(S5_GROUPS, S5_Q * S5_GROUP, S5_Q * S5_GROUP)
    er = lbr[:S5_Q][::-1].transpose(1, 0, 3, 2).reshape(S5_GROUPS, S5_Q * S5_GROUP, S5_STATE)
    ei = lbi[:S5_Q][::-1].transpose(1, 0, 3, 2).reshape(S5_GROUPS, S5_Q * S5_GROUP, S5_STATE)
    e_mat = jnp.concatenate([er, ei], axis=-1)
    clr = c_re[None] * pr[1:, :, None, :] - c_im[None] * pi[1:, :, None, :]
    cli = c_re[None] * pi[1:, :, None, :] + c_im[None] * pr[1:, :, None, :]
    fr = clr.transpose(1, 3, 0, 2).reshape(S5_GROUPS, S5_STATE, S5_Q * S5_GROUP)
    fi = (-cli).transpose(1, 3, 0, 2).reshape(S5_GROUPS, S5_STATE, S5_Q * S5_GROUP)
    f_mat = jnp.concatenate([fr, fi], axis=1)
    a1, a2 = [], []
    cur_r, cur_i = pr[S5_Q], pi[S5_Q]
    for _ in range(n_lev):
        a1.append(jnp.concatenate([cur_r, cur_r], axis=-1))
        a2.append(jnp.concatenate([-cur_i, cur_i], axis=-1))
        cur_r, cur_i = cur_r * cur_r - cur_i * cur_i, 2.0 * cur_r * cur_i
    pad = [jnp.zeros_like(a1[0])] * (8 - n_lev % 8 if n_lev % 8 else 0)
    a1 = jnp.stack(a1 + pad, axis=1)
    a2 = jnp.stack(a2 + pad, axis=1)
    return toep.astype(BF16), e_mat.astype(BF16), f_mat.astype(BF16), a1, a2


def _s5_conv_kernel(u_ref, t_ref, e_ref, f_ref, a1_ref, a2_ref, y_ref, *, seg, n_lev):
    u = u_ref[0]
    y1 = jnp.dot(u, t_ref[0], preferred_element_type=F32)
    x = jnp.dot(u, e_ref[0], preferred_element_type=F32)
    rin = lax.broadcasted_iota(jnp.int32, x.shape, 0) % seg
    for lev in range(n_lev):
        s = 1 << lev
        xs = jnp.where(rin >= s, pltpu.roll(x, s, 0), 0.0)
        x = x + a1_ref[0, lev:lev + 1, :] * xs + a2_ref[0, lev:lev + 1, :] * pltpu.roll(xs, S5_STATE, 1)
    xp = jnp.where(rin >= 1, pltpu.roll(x, 1, 0), 0.0)
    y_ref[0] = y1 + _mm(xp, f_ref[0])


def _s5_conv(u_grp, toep, e_mat, f_mat, a1, a2, seg, n_lev):
    g, m, w = u_grp.shape
    kern = functools.partial(_s5_conv_kernel, seg=seg, n_lev=n_lev)
    per_g = lambda a: pl.BlockSpec((1,) + a.shape[1:], lambda i: (i, 0, 0))
    return pl.pallas_call(
        kern,
        grid=(g,),
        in_specs=[per_g(u_grp), per_g(toep), per_g(e_mat), per_g(f_mat), per_g(a1), per_g(a2)],
        out_specs=pl.BlockSpec((1, m, w), lambda i: (i, 0, 0)),
        out_shape=jax.ShapeDtypeStruct((g, m, w), F32),
        compiler_params=_cparams("parallel"),
        name="s5_conv",
    )(u_grp, toep, e_mat, f_mat, a1, a2)


def _s5_glu_kernel(y_ref, u_ref, d_ref, w_ref, b_ref, o_ref):
    y = y_ref[...] + d_ref[...] * u_ref[...]
    y = 0.5 * y * (1.0 + lax.erf(y * (2.0 ** -0.5)))
    ab = _mm(y, w_ref[...]) + b_ref[...]
    o_ref[...] = (ab[:, :S5_WIDTH] * jax.nn.sigmoid(ab[:, S5_WIDTH:])).astype(o_ref.dtype)


def _s5_glu(yconv, u, d_skip, glu_w, glu_b):
    t_rows = u.shape[0]
    tm = _row_tile(t_rows, 512)
    row = pl.BlockSpec((tm, S5_WIDTH), lambda i: (i, 0))
    full = lambda a: pl.BlockSpec(a.shape, lambda i: (0,) * a.ndim)
    return pl.pallas_call(
        _s5_glu_kernel,
        grid=(t_rows // tm,),
        in_specs=[row, row, full(d_skip), full(glu_w), full(glu_b)],
        out_specs=row,
        out_shape=jax.ShapeDtypeStruct((t_rows, S5_WIDTH), BF16),
        compiler_params=_cparams("parallel"),
        name="s5_glu",
    )(yconv, u, d_skip, glu_w, glu_b)


def _proj_res_kernel(*refs, n_in):
    x_ref, g_ref = refs[0], refs[1]
    ys = refs[2:2 + n_in]
    ws = refs[2 + n_in:2 + 2 * n_in]
    o_ref = refs[2 + 2 * n_in]
    acc = jnp.dot(ys[0][...], ws[0][...], preferred_element_type=F32)
    for y, w in zip(ys[1:], ws[1:]):
        acc = acc + jnp.dot(y[...], w[...], preferred_element_type=F32)
    o_ref[...] = x_ref[...] + g_ref[0] * acc


def _proj_res(x, gate, seq, ys, w):
    t_rows = x.shape[0]
    tm = _row_tile(seq, 512)
    n_in = len(ys)
    width = ys[0].shape[1]
    assert all(y.shape[1] == width for y in ys) and n_in * width == w.shape[0]
    return pl.pallas_call(
        functools.partial(_proj_res_kernel, n_in=n_in),
        grid=(t_rows // tm,),
        in_specs=([pl.BlockSpec((tm, D_MODEL), lambda i: (i, 0)), _mod_spec(tm, seq)]
                  + [pl.BlockSpec((tm, width), lambda i: (i, 0)) for _ in ys]
                  + [pl.BlockSpec((width, D_MODEL), lambda i, j=j: (j, 0)) for j in range(n_in)]),
        out_specs=pl.BlockSpec((tm, D_MODEL), lambda i: (i, 0)),
        out_shape=jax.ShapeDtypeStruct(x.shape, F32),
        compiler_params=_cparams("parallel"),
        name="proj_res",
    )(x, gate, *ys, *([w] * n_in))


def _proj_res_t_kernel(x_ref, g_ref, yt_ref, w_ref, o_ref):
    acc = lax.dot_general(yt_ref[...], w_ref[...], (((0,), (0,)), ((), ())), preferred_element_type=F32)
    o_ref[...] = x_ref[...] + g_ref[0] * acc


def _proj_res_t(x, gate, seq, yt, w):
    t_rows = x.shape[0]
    tm = _row_tile(seq, 512)
    row = pl.BlockSpec((tm, D_MODEL), lambda i: (i, 0))
    return pl.pallas_call(
        _proj_res_t_kernel,
        grid=(t_rows // tm,),
        in_specs=[row, _mod_spec(tm, seq), pl.BlockSpec((yt.shape[0], tm), lambda i: (0, i)),
                  pl.BlockSpec(w.shape, lambda i: (0, 0))],
        out_specs=row,
        out_shape=jax.ShapeDtypeStruct(x.shape, F32),
        compiler_params=_cparams("parallel"),
        name="proj_res_t",
    )(x, gate, yt, w)


def _ffn_kernel(x_ref, nw_ref, sc_ref, sh_ref, g_ref, w1_ref, w3_ref, w2_ref, o_ref):
    x = x_ref[...]
    h = _norm_mod(x, nw_ref[...], sc_ref[0], sh_ref[0]).astype(BF16)
    a = jnp.dot(h, w1_ref[...], preferred_element_type=F32)
    b = jnp.dot(h, w3_ref[...], preferred_element_type=F32)
    f = jnp.dot((_silu(a) * b).astype(BF16), w2_ref[...], preferred_element_type=F32)
    o_ref[...] = x + g_ref[0] * f


def _ffn(x, seq, nw, sc, sh, gate, w13, w2):
    t_rows = x.shape[0]
    tm = _row_tile(seq, 256)
    row = pl.BlockSpec((tm, D_MODEL), lambda i: (i, 0))
    once = lambda a: pl.BlockSpec(a.shape, lambda i: (0,) * a.ndim, pipeline_mode=pl.Buffered(1))
    half = lambda j: pl.BlockSpec((D_MODEL, D_FF), lambda i: (0, j), pipeline_mode=pl.Buffered(1))
    return pl.pallas_call(
        _ffn_kernel,
        grid=(t_rows // tm,),
        in_specs=[row, once(nw), _mod_spec(tm, seq), _mod_spec(tm, seq), _mod_spec(tm, seq),
                  half(0), half(1), once(w2)],
        out_specs=row,
        out_shape=jax.ShapeDtypeStruct(x.shape, F32),
        compiler_params=_cparams("parallel"),
        name="ffn",
    )(x, nw, sc, sh, gate, w13, w13, w2)


LOG2E = 1.4426950408889634


def _odd_qkv_kernel(x_ref, nw_ref, sc_ref, sh_ref, w_ref, wvt_ref, qw_ref, kw_ref, q_ref, k_ref, vt_ref):
    h = _norm_mod(x_ref[...], nw_ref[...], sc_ref[0], sh_ref[0]).astype(BF16)
    proj = jnp.dot(h, w_ref[...], preferred_element_type=F32)
    r = lax.broadcasted_iota(jnp.int32, (LANES, LANES), 0) // DIFF_HEAD_DIM
    c = lax.broadcasted_iota(jnp.int32, (LANES, LANES), 1) // DIFF_HEAD_DIM
    grp = (r == c).astype(BF16)
    q_scale = DIFF_HEAD_DIM ** -0.5 * LOG2E
    for base, w_vec, scale, out in ((0, qw_ref, q_scale, q_ref), (D_MODEL, kw_ref, 1.0, k_ref)):
        for t in range(D_MODEL // LANES):
            xt = proj[:, base + t * LANES:base + (t + 1) * LANES]
            ss = _mm(xt * xt, grp)
            y = xt * lax.rsqrt(ss * (1.0 / DIFF_HEAD_DIM) + EPS) * w_vec[...]
            out[:, t * LANES:(t + 1) * LANES] = (y * scale).astype(out.dtype)
    vt_ref[...] = _mm_nt(wvt_ref[...], h).astype(vt_ref.dtype)


def _odd_qkv(x, seq, nw, sc, sh, w_qk, w_vt, qw, kw):
    t_rows = x.shape[0]
    tm = _row_tile(seq, 512)
    row = pl.BlockSpec((tm, D_MODEL), lambda i: (i, 0))
    full = lambda a: pl.BlockSpec(a.shape, lambda i: (0,) * a.ndim)
    out = jax.ShapeDtypeStruct((t_rows, D_MODEL), BF16)
    return pl.pallas_call(
        _odd_qkv_kernel,
        grid=(t_rows // tm,),
        in_specs=[row, full(nw), _mod_spec(tm, seq), _mod_spec(tm, seq),
                  pl.BlockSpec((D_MODEL, 2 * D_MODEL), lambda i: (0, 0)), full(w_vt), full(qw),
                  full(kw)],
        out_specs=[row, row, pl.BlockSpec((D_MODEL, tm), lambda i: (0, i))],
        out_shape=[out, out, jax.ShapeDtypeStruct((D_MODEL, t_rows), BF16)],
        compiler_params=_cparams("parallel"),
        name="odd_qkv",
    )(x, nw, sc, sh, w_qk, w_vt, qw, kw)


def _attn_kernel(bound_ref, q_ref, k_ref, vt_ref, lam_ref, sw_ref, o_ref, acc_scr, pp_scr, l_scr,
                 *, tq, tk, heads, out_scale):
    i = pl.program_id(2)
    lane = lax.broadcasted_iota(jnp.int32, (tq, LANES), 1)
    qs = []
    for hh in range(heads):
        q = q_ref[:, hh * LANES:(hh + 1) * LANES]
        zero = jnp.zeros_like(q)
        qs += [jnp.where(lane < DIFF_HEAD_DIM, q, zero), jnp.where(lane >= DIFF_HEAD_DIM, q, zero)]
    n_sub = tq // LANES
    n_full = i * (tq // tk)

    def load_k(j, hh):
        k0 = pl.multiple_of(j * tk, tk)
        return k0, k_ref[pl.ds(k0, tk), hh * LANES:(hh + 1) * LANES]

    def load_v(j, hh):
        return vt_ref[hh * LANES:(hh + 1) * LANES, pl.ds(pl.multiple_of(j * tk, tk), tk)]

    def scores(kb, u, c, k0, masked):
        s = lax.dot_general(kb, qs[u][c * LANES:(c + 1) * LANES, :], (((1,), (1,)), ((), ())),
                            preferred_element_type=F32)
        if masked:
            kc = (k0 + lax.broadcasted_iota(jnp.int32, (tk, LANES), 0)) // CHUNK
            qc = (i * tq + c * LANES + lax.broadcasted_iota(jnp.int32, (tk, LANES), 1)) // CHUNK
            s = jnp.where(kc <= qc, s, -1e30)
        return s

    def finish(ls):
        lam = lam_ref[0:1, 0:1]
        for hh in range(heads):
            o = acc_scr[2 * hh] / ls[2 * hh] - lam * (acc_scr[2 * hh + 1] / ls[2 * hh + 1])
            o = o * lax.rsqrt(jnp.mean(o * o, axis=0, keepdims=True) + EPS) * sw_ref[...] * out_scale
            o_ref[hh * LANES:(hh + 1) * LANES, :] = o.astype(o_ref.dtype)

    acc_scr[...] = jnp.zeros_like(acc_scr)

    @pl.when(bound_ref[0] == 1)
    def _():
        l_scr[...] = jnp.zeros_like(l_scr)
        pp_scr[1] = jnp.zeros(pp_scr.shape[1:], BF16)

        def values(j, hh):
            vb = load_v(j, hh)
            for u in (2 * hh, 2 * hh + 1):
                acc_scr[u] = acc_scr[u] + jnp.dot(vb, pp_scr[j % 2, u], preferred_element_type=F32)

        def visibility(jj, c):
            k_lo, k_hi = (jj * tk) // CHUNK, ((jj + 1) * tk - 1) // CHUNK
            q_lo, q_hi = (c * LANES) // CHUNK, ((c + 1) * LANES - 1) // CHUNK
            return "full" if k_hi <= q_lo else ("none" if k_lo > q_hi else "part")

        def step(j, jj):
            jp = jnp.maximum(j - 1, 0) + (j == 0).astype(jnp.int32)
            for hh in range(heads):
                values(jp, hh)
                k0, kb = load_k(j, hh)
                for u in (2 * hh, 2 * hh + 1):
                    for c in range(n_sub):
                        cs = slice(c * LANES, (c + 1) * LANES)
                        vis = "full" if jj is None else visibility(jj, c)
                        if vis == "none":
                            pp_scr[j % 2, u, :, cs] = jnp.zeros((tk, LANES), BF16)
                            continue
                        p = jnp.exp2(scores(kb, u, c, k0, vis == "part"))
                        l_scr[u, :, cs] = l_scr[u, :, cs] + jnp.sum(p.reshape(tk // 8, 8, LANES), axis=0)
                        pp_scr[j % 2, u, :, cs] = p.astype(BF16)

        def body(j, carry):
            step(j, None)
            return carry

        lax.fori_loop(0, n_full, body, 0)
        n_mask = tq // tk
        for jj in range(n_mask):
            step(n_full + jj, jj)
        for hh in range(heads):
            values(n_full + n_mask - 1, hh)
        finish([jnp.sum(l_scr[u], axis=0, keepdims=True) for u in range(2 * heads)])

    @pl.when(bound_ref[0] != 1)
    def _():
        neg = jnp.full((1, tq), -1e30, F32)
        zer = jnp.zeros((1, tq), F32)

        def block(j, carry, masked):
            out = []
            for hh in range(heads):
                k0, kb = load_k(j, hh)
                vb = load_v(j, hh)
                alphas = []
                for u in (2 * hh, 2 * hh + 1):
                    m, l = carry[2 * u], carry[2 * u + 1]
                    m_parts, l_parts, a_parts = [], [], []
                    for c in range(n_sub):
                        cs = slice(c * LANES, (c + 1) * LANES)
                        s = scores(kb, u, c, k0, masked)
                        m_new = jnp.maximum(m[:, cs], jnp.max(s, axis=0, keepdims=True))
                        p = jnp.exp2(s - m_new)
                        alpha = jnp.exp2(m[:, cs] - m_new)
                        l_parts.append(alpha * l[:, cs] + jnp.sum(p, axis=0, keepdims=True))
                        pp_scr[0, u, :, cs] = p.astype(BF16)
                        m_parts.append(m_new)
                        a_parts.append(alpha)
                    out += [jnp.concatenate(m_parts, axis=1), jnp.concatenate(l_parts, axis=1)]
                    alphas.append(jnp.concatenate(a_parts, axis=1))
                for t, u in enumerate((2 * hh, 2 * hh + 1)):
                    acc_scr[u] = alphas[t] * acc_scr[u] + jnp.dot(vb, pp_scr[0, u], preferred_element_type=F32)
            return tuple(out)

        carry = lax.fori_loop(0, n_full, lambda j, c: block(j, c, False), (neg, zer) * (2 * heads))
        for jj in range(tq // tk):
            carry = block(n_full + jj, carry, True)
        finish([carry[2 * u + 1] for u in range(2 * heads)])


ATTN_HEADS_PER_STEP = 4
SCORE_BOUND_LOG2 = 64.0


def _attention(q, k, vt, score_bound, lam_vec, subln_col, batch, seq, out_scale):
    tq = _row_tile(seq, 512)
    tk = _row_tile(tq, 256)
    nq = seq // tq
    assert seq // tk >= 2
    hp = ATTN_HEADS_PER_STEP
    wid = hp * LANES
    kern = functools.partial(_attn_kernel, tq=tq, tk=tk, heads=hp, out_scale=out_scale)
    full = lambda a: pl.BlockSpec(a.shape, lambda b, h, i: (0,) * a.ndim)
    bounded = (score_bound <= SCORE_BOUND_LOG2).astype(jnp.int32).reshape(1)
    return pl.pallas_call(
        kern,
        grid=(batch, DIFF_HEADS // hp, nq),
        in_specs=[pl.BlockSpec(memory_space=pltpu.SMEM),
                  pl.BlockSpec((tq, wid), lambda b, h, i: (b * nq + i, h)),
                  pl.BlockSpec((seq, wid), lambda b, h, i: (b, h)),
                  pl.BlockSpec((wid, seq), lambda b, h, i: (h, b)),
                  full(lam_vec), full(subln_col)],
        out_specs=pl.BlockSpec((wid, tq), lambda b, h, i: (h, b * nq + i)),
        out_shape=jax.ShapeDtypeStruct(vt.shape, BF16),
        scratch_shapes=[pltpu.VMEM((2 * hp, LANES, tq), F32),
                        pltpu.VMEM((2, 2 * hp, tk, tq), BF16), pltpu.VMEM((2 * hp, 8, tq), F32)],
        compiler_params=_cparams("parallel", "parallel", "arbitrary"),
        name="diff_attn",
    )(bounded, q, k, vt, lam_vec, subln_col)


def _router_kernel(x_ref, nw_ref, sc_ref, sh_ref, whi_ref, wlo_ref, h_ref, idx_ref, gate_ref):
    h = _norm_mod(x_ref[...], nw_ref[...], sc_ref[0], sh_ref[0])
    h_ref[...] = h
    h_hi = h.astype(BF16)
    h_lo = (h - h_hi.astype(F32)).astype(BF16)
    logits = (jnp.dot(h_hi, whi_ref[...], preferred_element_type=F32)
              + jnp.dot(h_hi, wlo_ref[...], preferred_element_type=F32)
              + jnp.dot(h_lo, whi_ref[...], preferred_element_type=F32))
    lane = lax.broadcasted_iota(jnp.int32, logits.shape, 1)
    lane_f = lane.astype(F32)
    logits = jnp.where(lane < N_EXPERTS, logits, -jnp.inf)
    m1 = jnp.max(logits, axis=-1, keepdims=True)
    i1 = jnp.min(jnp.where(logits == m1, lane_f, float(LANES)), axis=-1, keepdims=True)
    rest = jnp.where(lane_f == i1, -jnp.inf, logits)
    m2 = jnp.max(rest, axis=-1, keepdims=True)
    i2 = jnp.min(jnp.where(rest == m2, lane_f, float(LANES)), axis=-1, keepdims=True)
    e = jnp.exp(m2 - m1)
    g1 = 1.0 / (1.0 + e)
    idx_ref[...] = jnp.where(lane == 0, i1, jnp.where(lane == 1, i2, 0.0)).astype(jnp.int32)
    gate_ref[...] = jnp.where(lane == 0, g1, jnp.where(lane == 1, e * g1, 0.0))


def _router(x, seq, nw, sc, sh, w_hi, w_lo):
    t_rows = x.shape[0]
    tm = _row_tile(seq, 512)
    row = lambda w: pl.BlockSpec((tm, w), lambda i: (i, 0))
    full = lambda a: pl.BlockSpec(a.shape, lambda i: (0,) * a.ndim)
    return pl.pallas_call(
        _router_kernel,
        grid=(t_rows // tm,),
        in_specs=[row(D_MODEL), full(nw), _mod_spec(tm, seq), _mod_spec(tm, seq), full(w_hi), full(w_lo)],
        out_specs=[row(D_MODEL), row(LANES), row(LANES)],
        out_shape=[jax.ShapeDtypeStruct((t_rows, D_MODEL), F32),
                   jax.ShapeDtypeStruct((t_rows, LANES), jnp.int32),
                   jax.ShapeDtypeStruct((t_rows, LANES), F32)],
        compiler_params=_cparams("parallel"),
        name="router",
    )(x, nw, sc, sh, w_hi, w_lo)


def _start_row_gather(idx_ref, src_hbm, dst_ref, sem, n_rows, inline=False):
    def start(r2, _):
        for prio in range(2):
            r = 2 * r2 + prio
            pltpu.make_async_copy(src_hbm.at[pl.ds(idx_ref[0, 0, r], 1), :], dst_ref.at[pl.ds(r, 1), :],
                                  sem).start(priority=prio)
        return 0

    if inline:
        for r2 in range(n_rows // 2):
            start(r2, 0)
    else:
        lax.fori_loop(0, n_rows // 2, start, 0, unroll=4)


def _wait_row_gather(src_hbm, dst_ref, sem, n_rows):
    pltpu.make_async_copy(src_hbm.at[pl.ds(0, n_rows), :], dst_ref, sem).wait()


def _prefetched_rows(i, idx_cur_ref, idx_next_ref, src_hbm, buf, sems, n_rows):
    slot = i % 2

    @pl.when(i == 0)
    def _():
        _start_row_gather(idx_cur_ref, src_hbm, buf.at[0], sems.at[0], n_rows)

    @pl.when(i + 1 < pl.num_programs(0))
    def _():
        _start_row_gather(idx_next_ref, src_hbm, buf.at[1 - slot], sems.at[1 - slot], n_rows)

    _wait_row_gather(src_hbm, buf.at[slot], sems.at[slot], n_rows)
    return buf.at[slot]


def _expert_kernel(te_ref, idx_cur_ref, idx_next_ref, h_hbm, w1_ref, w3_ref, w2_ref, o_ref, xbuf, sems):
    i = pl.program_id(0)
    last = pl.num_programs(0) - 1
    slot = i % 2

    @pl.when(i == 0)
    def _():
        _start_row_gather(idx_cur_ref, h_hbm, xbuf.at[0], sems.at[0], EXPERT_TM)

    _wait_row_gather(h_hbm, xbuf.at[slot], sems.at[slot], EXPERT_TM)
    _start_row_gather(idx_next_ref, h_hbm, xbuf.at[1 - slot], sems.at[1 - slot], EXPERT_TM, inline=True)
    h = xbuf[slot].astype(BF16)
    a = jnp.dot(h, w1_ref[0], preferred_element_type=F32)
    b = jnp.dot(h, w3_ref[0], preferred_element_type=F32)
    o_ref[...] = jnp.dot((_silu(a) * b).astype(BF16), w2_ref[0], preferred_element_type=F32)

    @pl.when(i == last)
    def _():
        _wait_row_gather(h_hbm, xbuf.at[1 - slot], sems.at[1 - slot], EXPERT_TM)


EXPERT_TM = 256


def _expert_ffn(h, tok_of_slot, tile_expert, w13, w2):
    tm = EXPERT_TM
    n_tiles = tok_of_slot.shape[0] // tm
    idx = tok_of_slot.reshape(n_tiles, 1, tm)
    return pl.pallas_call(
        _expert_kernel,
        grid_spec=pltpu.PrefetchScalarGridSpec(
            num_scalar_prefetch=1,
            grid=(n_tiles,),
            in_specs=[pl.BlockSpec((1, 1, tm), lambda i, te: (i, 0, 0), memory_space=pltpu.SMEM),
                      pl.BlockSpec((1, 1, tm), lambda i, te: (jnp.minimum(i + 1, n_tiles - 1), 0, 0),
                                   memory_space=pltpu.SMEM),
                      pl.BlockSpec(memory_space=pl.ANY),
                      pl.BlockSpec((1, D_MODEL, D_FF), lambda i, te: (te[i], 0, 0)),
                      pl.BlockSpec((1, D_MODEL, D_FF), lambda i, te: (te[i], 0, 1)),
                      pl.BlockSpec((1, D_FF, D_MODEL), lambda i, te: (te[i], 0, 0))],
            out_specs=pl.BlockSpec((tm, D_MODEL), lambda i, te: (i, 0)),
            scratch_shapes=[pltpu.VMEM((2, tm, D_MODEL), F32), pltpu.SemaphoreType.DMA((2,))],
        ),
        out_shape=jax.ShapeDtypeStruct((n_tiles * tm, D_MODEL), F32),
        compiler_params=_cparams("arbitrary"),
        name="expert_ffn",
    )(tile_expert, idx, idx, h, w13, w13, w2)


COMBINE_TM = 256


def _combine_kernel(p0_cur, p0_next, p1_cur, p1_next, x_ref, g_ref, gate_ref, y_hbm, o_ref, buf0, buf1, sems0,
                    sems1):
    i = pl.program_id(0)
    y0 = _prefetched_rows(i, p0_cur, p0_next, y_hbm, buf0, sems0, COMBINE_TM)
    y1 = _prefetched_rows(i, p1_cur, p1_next, y_hbm, buf1, sems1, COMBINE_TM)
    gates = gate_ref[...]
    f = gates[:, 0:1] * y0[...] + gates[:, 1:2] * y1[...]
    o_ref[...] = x_ref[...] + g_ref[0] * f


def _combine(x, gate_mod, seq, gates, y_perm, pos):
    t_rows = x.shape[0]
    tm = COMBINE_TM
    nt = t_rows // tm
    row = pl.BlockSpec((tm, D_MODEL), lambda i: (i, 0))
    cur = pl.BlockSpec((1, 1, tm), lambda i: (i, 0, 0), memory_space=pltpu.SMEM)
    nxt = pl.BlockSpec((1, 1, tm), lambda i: (jnp.minimum(i + 1, nt - 1), 0, 0), memory_space=pltpu.SMEM)
    p0 = pos[:, 0].reshape(nt, 1, tm)
    p1 = pos[:, 1].reshape(nt, 1, tm)
    return pl.pallas_call(
        _combine_kernel,
        grid=(nt,),
        in_specs=[cur, nxt, cur, nxt, row, _mod_spec(tm, seq), pl.BlockSpec((tm, LANES), lambda i: (i, 0)),
                  pl.BlockSpec(memory_space=pl.ANY)],
        out_specs=row,
        out_shape=jax.ShapeDtypeStruct(x.shape, F32),
        scratch_shapes=[pltpu.VMEM((2, tm, D_MODEL), F32), pltpu.VMEM((2, tm, D_MODEL), F32),
                        pltpu.SemaphoreType.DMA((2,)), pltpu.SemaphoreType.DMA((2,))],
        compiler_params=_cparams("arbitrary"),
        name="moe_combine",
    )(p0, p0, p1, p1, x, gate_mod, gates, y_perm)


def _dispatch_plan(expert_idx, t_rows):
    tm = EXPERT_TM
    flat = expert_idx.reshape(-1)
    onehot = (flat[:, None] == jnp.arange(N_EXPERTS)[None, :]).astype(jnp.int32)
    rank = jnp.take_along_axis(jnp.cumsum(onehot, axis=0) - onehot, flat[:, None], axis=1)[:, 0]
    counts = jnp.sum(onehot, axis=0)
    tiles_per = (counts + tm - 1) // tm
    tile_end = jnp.cumsum(tiles_per)
    offs = (tile_end - tiles_per) * tm
    pos = offs[flat] + rank
    n_tiles = (2 * t_rows) // tm + N_EXPERTS
    p_rows = n_tiles * tm
    tok_of_slot = jnp.zeros((p_rows,), jnp.int32).at[pos].set(jnp.arange(2 * t_rows, dtype=jnp.int32) // 2)
    tile_expert = jnp.minimum(
        jnp.sum(jnp.arange(n_tiles, dtype=jnp.int32)[:, None] >= tile_end[None, :], axis=1), N_EXPERTS - 1
    ).astype(jnp.int32)
    return tok_of_slot, tile_expert, pos.reshape(t_rows, 2).astype(jnp.int32)


def _mods(mod, layer, batch):
    m = mod[layer, :batch].reshape(batch, 6, 1, D_MODEL)
    return [m[:, j] for j in range(6)]


def _even_layer(x, batch, seq, mods, nw_mix, nw_ffn, w_in, conv_w, a_log, dt_bias, gdn_norm_w,
                lam_re, lam_im, log_step, b_re, b_im, c_re, c_im, d_skip, glu_w, glu_b, w_out, w13, w2):
    sh1, sc1, g1, sh2, sc2, g2 = mods
    t_rows = batch * seq
    qkvz, rest = w_in[:, :4 * GDN_WIDTH], w_in[:, 4 * GDN_WIDTH:]
    w_ba, w_u = rest[:, :2 * GDN_HEADS], rest[:, 2 * GDN_HEADS:]
    w_cat = jnp.concatenate(
        [qkvz, w_u, w_ba, jnp.zeros((D_MODEL, LANES - 2 * GDN_HEADS), w_in.dtype)], axis=1).astype(BF16)
    gate_params = jnp.zeros((8, LANES), F32)
    gate_params = gate_params.at[0, GDN_HEADS:2 * GDN_HEADS].set(a_log.astype(F32))
    gate_params = gate_params.at[1, GDN_HEADS:2 * GDN_HEADS].set(dt_bias.astype(F32))
    q, k, v, z, u, gb = _even_in(x, seq, nw_mix, sc1, sh1, w_cat, conv_w.astype(F32), gate_params)
    y_a = _gdn(q, k, v, z, gb, gdn_norm_w.reshape(1, GDN_HEAD_DIM).astype(F32), batch, seq)

    seg = seq // S5_Q
    n_lev = max(int(math.ceil(math.log2(seg))), 0)
    toep, e_mat, f_mat, a1, a2 = _s5_tables(lam_re, lam_im, log_step, b_re, b_im, c_re, c_im, max(n_lev, 1))
    m_rows = t_rows // S5_Q
    u_grp = (u.reshape(m_rows, S5_Q, S5_GROUPS, S5_GROUP).transpose(2, 0, 1, 3)
             .reshape(S5_GROUPS, m_rows, S5_Q * S5_GROUP).astype(BF16))
    yg = _s5_conv(u_grp, toep, e_mat, f_mat, a1, a2, seg, n_lev)
    yconv = (yg.reshape(S5_GROUPS, m_rows, S5_Q, S5_GROUP).transpose(1, 2, 0, 3)
             .reshape(t_rows, S5_WIDTH))
    y_b = _s5_glu(yconv, u, d_skip.reshape(1, S5_WIDTH).astype(F32), glu_w.astype(BF16),
                  glu_b.reshape(1, 2 * S5_WIDTH).astype(F32))
    wo = w_out.astype(BF16)
    x = _proj_res(x, g1, seq, [y_a, y_b], wo)
    return _ffn(x, seq, nw_ffn, sc2, sh2, g2, w13.astype(BF16), w2.astype(BF16))


def _odd_layer(x, batch, seq, mods, nw_mix, nw_ffn, w_qkv, q_norm_w, k_norm_w, lq1, lk1, lq2, lk2,
               subln_w, w_out, router_w, w13, w2, lambda_init):
    sh1, sc1, g1, sh2, sc2, g2 = mods
    t_rows = batch * seq
    tile2 = lambda w: jnp.tile(w.astype(F32), LANES // DIFF_HEAD_DIM).reshape(1, LANES)
    w_qkv_b = w_qkv.astype(BF16)
    q, k, vt = _odd_qkv(x, seq, nw_mix, sc1, sh1, w_qkv_b, w_qkv_b[:, 2 * D_MODEL:].T,
                        tile2(q_norm_w), tile2(k_norm_w))
    lam = (jnp.exp(jnp.sum(lq1.astype(F32) * lk1.astype(F32)))
           - jnp.exp(jnp.sum(lq2.astype(F32) * lk2.astype(F32))) + lambda_init)
    lam_vec = jnp.full((8, LANES), lam, F32)
    score_bound = (1.02 * LOG2E * DIFF_HEAD_DIM ** 0.5
                   * jnp.max(jnp.abs(q_norm_w.astype(F32))) * jnp.max(jnp.abs(k_norm_w.astype(F32))))
    ot = _attention(q, k, vt, score_bound, lam_vec, subln_w.reshape(LANES, 1).astype(F32), batch, seq,
                    1.0 - lambda_init)
    x = _proj_res_t(x, g1, seq, ot, w_out.astype(BF16))

    rw = jnp.zeros((D_MODEL, LANES), F32).at[:, :N_EXPERTS].set(router_w.astype(F32))
    rw_hi = rw.astype(BF16)
    rw_lo = (rw - rw_hi.astype(F32)).astype(BF16)
    h, idx, gates = _router(x, seq, nw_ffn, sc2, sh2, rw_hi, rw_lo)
    tok_of_slot, tile_expert, pos = _dispatch_plan(idx[:, :2], t_rows)
    y_perm = _expert_ffn(h, tok_of_slot, tile_expert, w13.astype(BF16), w2.astype(BF16))
    return _combine(x, g2, seq, gates, y_perm, pos)


def kernel(x, c, ada_w, ada_b, norm_mix_w, norm_ffn_w, even_w_in, even_conv_w, even_a_log, even_dt_bias, even_gdn_norm_w, even_lam_re, even_lam_im, even_log_step, even_b_re, even_b_im, even_c_re, even_c_im, even_d_skip, even_glu_w, even_glu_b, even_w_out, even_ffn_w13, even_ffn_w2, odd_w_qkv, odd_q_norm_w, odd_k_norm_w, odd_lambda_q1, odd_lambda_k1, odd_lambda_q2, odd_lambda_k2, odd_subln_w, odd_w_out, odd_router_w, odd_expert_w13, odd_expert_w2):
    batch, seq, d = x.shape
    assert d == D_MODEL and seq % CHUNK == 0 and seq % S5_Q == 0
    c_pad = jnp.zeros((8, d), F32).at[:batch].set(c.astype(F32))
    mod = _adaln(c_pad, ada_w.astype(BF16), ada_b.astype(F32))
    xf = x.astype(F32).reshape(batch * seq, d)
    for layer in range(DEPTH):
        i = layer // 2
        mods = _mods(mod, layer, batch)
        nw_mix = norm_mix_w[layer].reshape(1, d).astype(F32)
        nw_ffn = norm_ffn_w[layer].reshape(1, d).astype(F32)
        if layer % 2 == 0:
            xf = _even_layer(xf, batch, seq, mods, nw_mix, nw_ffn, even_w_in[i], even_conv_w[i], even_a_log[i],
                             even_dt_bias[i], even_gdn_norm_w[i], even_lam_re[i], even_lam_im[i],
                             even_log_step[i], even_b_re[i], even_b_im[i], even_c_re[i], even_c_im[i],
                             even_d_skip[i], even_glu_w[i], even_glu_b[i], even_w_out[i],
                             even_ffn_w13[i], even_ffn_w2[i])
        else:
            lambda_init = 0.8 - 0.6 * math.exp(-0.3 * layer)
            xf = _odd_layer(xf, batch, seq, mods, nw_mix, nw_ffn, odd_w_qkv[i], odd_q_norm_w[i], odd_k_norm_w[i],
                            odd_lambda_q1[i], odd_lambda_k1[i], odd_lambda_q2[i], odd_lambda_k2[i],
                            odd_subln_w[i], odd_w_out[i], odd_router_w[i], odd_expert_w13[i],
                            odd_expert_w2[i], lambda_init)
    return xf.reshape(batch, seq, d).astype(x.dtype)
```

```python
import functools
import math

import jax
import jax.numpy as jnp
from jax import lax
from jax.experimental import pallas as pl
from jax.experimental.pallas import tpu as pltpu

F32 = jnp.float32
BF16 = jnp.bfloat16

D_MODEL = 1024
DEPTH = 4
CHUNK = 64
EPS = 1e-6
GDN_HEAD_DIM = 128
GDN_WIDTH = D_MODEL // 2
GDN_HEADS = GDN_WIDTH // GDN_HEAD_DIM
CONV_K = 4
QKV_W = 3 * GDN_WIDTH
S5_WIDTH = D_MODEL - GDN_WIDTH
S5_GROUP = 16
S5_GROUPS = S5_WIDTH // S5_GROUP
S5_STATE = 64
S5_Q = 16
DIFF_HEAD_DIM = 64
DIFF_HEADS = D_MODEL // (2 * DIFF_HEAD_DIM)
D_FF = ((8 * D_MODEL // 3 + 127) // 128) * 128
N_EXPERTS = 8
LANES = 128
EVEN_IN_COLS = QKV_W + GDN_WIDTH + S5_WIDTH + LANES
VMEM_LIMIT = 56 * 1024 * 1024


def _cparams(*sem):
    return pltpu.CompilerParams(dimension_semantics=sem, vmem_limit_bytes=VMEM_LIMIT)


def _mm(a, b):
    return jnp.dot(a.astype(BF16), b.astype(BF16), preferred_element_type=F32)


def _mm_nt(a, b):
    return lax.dot_general(a.astype(BF16), b.astype(BF16), (((1,), (1,)), ((), ())),
                           preferred_element_type=F32)


def _mm_tn(a, b):
    return lax.dot_general(a.astype(BF16), b.astype(BF16), (((0,), (0,)), ((), ())),
                           preferred_element_type=F32)


def _mm_hi(a, b):
    a_hi = a.astype(BF16)
    a_lo = (a - a_hi.astype(F32)).astype(BF16)
    b_hi = b.astype(BF16)
    b_lo = (b - b_hi.astype(F32)).astype(BF16)
    return (jnp.dot(a_hi, b_hi, preferred_element_type=F32)
            + (jnp.dot(a_hi, b_lo, preferred_element_type=F32) + jnp.dot(a_lo, b_hi, preferred_element_type=F32)))


def _split3(x):
    hi = x.astype(BF16)
    r = x - hi.astype(F32)
    mid = r.astype(BF16)
    lo = (r - mid.astype(F32)).astype(BF16)
    return hi, mid, lo


def _norm_mod(x, nw, sc, sh):
    ms = jnp.mean(x * x, axis=-1, keepdims=True)
    return (x * lax.rsqrt(ms + EPS) * nw) * (1.0 + sc) + sh


def _silu(x):
    return x * jax.nn.sigmoid(x)


def _row_tile(n, pref):
    t = min(pref, n)
    while n % t:
        t //= 2
    return t


CAST_BLOCK_BYTES = 4 * 1024 * 1024


def _cast_kernel(x_ref, o_ref):
    o_ref[...] = x_ref[...].astype(o_ref.dtype)


def _to_bf16(w):
    cols = w.shape[-1]
    w2d = w.reshape(-1, cols)
    rows = w2d.shape[0]
    rb = 16
    assert rows % rb == 0 and cols % LANES == 0
    while 2 * rb * cols * 4 <= CAST_BLOCK_BYTES and rows % (2 * rb) == 0:
        rb *= 2
    out = pl.pallas_call(
        _cast_kernel,
        grid=(rows // rb,),
        in_specs=[pl.BlockSpec((rb, cols), lambda i: (i, 0))],
        out_specs=pl.BlockSpec((rb, cols), lambda i: (i, 0)),
        out_shape=jax.ShapeDtypeStruct((rows, cols), BF16),
        compiler_params=_cparams("parallel"),
        name="to_bf16",
    )(w2d)
    return out.reshape(w.shape)


def _adaln_kernel(c_ref, w_ref, b_ref, o_ref):
    c = c_ref[...]
    o_ref[0] = _mm(_silu(c), w_ref[0]) + b_ref[0]


def _adaln(c_pad, ada_w, ada_b):
    depth, d, n = ada_w.shape
    tn = 1536
    rows = c_pad.shape[0]
    return pl.pallas_call(
        _adaln_kernel,
        grid=(depth, n // tn),
        in_specs=[pl.BlockSpec((rows, d), lambda l, j: (0, 0)),
                  pl.BlockSpec((1, d, tn), lambda l, j: (l, 0, j)),
                  pl.BlockSpec((1, 1, tn), lambda l, j: (l, 0, j))],
        out_specs=pl.BlockSpec((1, rows, tn), lambda l, j: (l, 0, j)),
        out_shape=jax.ShapeDtypeStruct((depth, rows, n), F32),
        compiler_params=_cparams("parallel", "parallel"),
        name="adaln",
    )(c_pad, ada_w, ada_b.reshape(depth, 1, n))


def _mod_spec(tm, seq):
    return pl.BlockSpec((1, 1, D_MODEL), lambda i, *_: ((i * tm) // seq, 0, 0))


def _even_in_kernel(x_ref, xh_ref, nw_ref, sc_ref, sh_ref, w_ref, cw_ref, gp_ref,
                    q_ref, k_ref, v_ref, z_ref, u_ref, gb_ref, pre_scr, *, tm, seq):
    i = pl.program_id(0)
    nw = nw_ref[...]
    sc = sc_ref[0]
    sh = sh_ref[0]
    h = _norm_mod(x_ref[...], nw, sc, sh).astype(BF16)
    proj = jnp.dot(h, w_ref[...], preferred_element_type=F32)
    hh = _norm_mod(xh_ref[...], nw, sc, sh).astype(BF16)
    preh = jnp.dot(hh, w_ref[:, :QKV_W], preferred_element_type=F32)
    preh = jnp.where((i * tm) % seq == 0, 0.0, preh)
    pre = proj[:, :QKV_W]
    pre_scr[0:8, :] = preh
    pre_scr[8:8 + tm, :] = pre
    acc = pre * cw_ref[CONV_K - 1:CONV_K, :]
    for j in range(CONV_K - 1):
        acc = acc + pre_scr[pl.ds(8 - (CONV_K - 1) + j, tm), :] * cw_ref[j:j + 1, :]
    qkv = _silu(acc)
    for hd in range(GDN_HEADS):
        lo = hd * GDN_HEAD_DIM
        qh = qkv[:, lo:lo + GDN_HEAD_DIM]
        kh = qkv[:, GDN_WIDTH + lo:GDN_WIDTH + lo + GDN_HEAD_DIM]
        q_ref[:, lo:lo + GDN_HEAD_DIM] = (
            qh * lax.rsqrt(jnp.sum(qh * qh, axis=-1, keepdims=True) + EPS) * GDN_HEAD_DIM ** -0.5)
        k_ref[:, lo:lo + GDN_HEAD_DIM] = kh * lax.rsqrt(jnp.sum(kh * kh, axis=-1, keepdims=True) + EPS)
    v_ref[...] = qkv[:, 2 * GDN_WIDTH:]
    z_ref[...] = proj[:, QKV_W:QKV_W + GDN_WIDTH]
    u_ref[...] = proj[:, QKV_W + GDN_WIDTH:QKV_W + GDN_WIDTH + S5_WIDTH]
    ba = proj[:, QKV_W + GDN_WIDTH + S5_WIDTH:]
    lane = lax.broadcasted_iota(jnp.int32, ba.shape, 1)
    t = ba + gp_ref[1:2, :]
    softplus = jnp.maximum(t, 0.0) + jnp.log1p(jnp.exp(-jnp.abs(t)))
    g = -jnp.exp(gp_ref[0:1, :]) * softplus
    gb_ref[...] = jnp.where(lane < GDN_HEADS, jax.nn.sigmoid(ba), g)


def _even_in(x, seq, nw, sc, sh, w_cat, conv_w, gate_params):
    t_rows = x.shape[0]
    tm = _row_tile(seq, 512)
    kern = functools.partial(_even_in_kernel, tm=tm, seq=seq)
    row = lambda w: pl.BlockSpec((tm, w), lambda i: (i, 0))
    full = lambda a: pl.BlockSpec(a.shape, lambda i: (0,) * a.ndim)
    outs = [jax.ShapeDtypeStruct((t_rows, GDN_WIDTH), F32)] * 5 + [jax.ShapeDtypeStruct((t_rows, LANES), F32)]
    return pl.pallas_call(
        kern,
        grid=(t_rows // tm,),
        in_specs=[row(D_MODEL),
                  pl.BlockSpec((8, D_MODEL), lambda i: (jnp.maximum(i * (tm // 8) - 1, 0), 0)),
                  full(nw), _mod_spec(tm, seq), _mod_spec(tm, seq), full(w_cat), full(conv_w),
                  full(gate_params)],
        out_specs=[row(GDN_WIDTH)] * 5 + [row(LANES)],
        out_shape=outs,
        scratch_shapes=[pltpu.VMEM((tm + 8, QKV_W), F32)],
        compiler_params=_cparams("parallel"),
        name="even_in",
    )(x, x, nw, sc, sh, w_cat, conv_w, gate_params)


def _gdn_kernel(q_ref, k_ref, v_ref, z_ref, gb_ref, nw_ref, ya_ref, state_scr, *, batch):
    @pl.when(pl.program_id(0) == 0)
    def _():
        state_scr[...] = jnp.zeros_like(state_scr)

    r64 = lax.broadcasted_iota(jnp.int32, (CHUNK, CHUNK), 0)
    c64 = lax.broadcasted_iota(jnp.int32, (CHUNK, CHUNK), 1)
    causal = r64 >= c64
    strict = r64 > c64
    tri = causal.astype(BF16)
    sel = (lax.broadcasted_iota(jnp.int32, (8, LANES), 1)
           == lax.broadcasted_iota(jnp.int32, (8, LANES), 0) + GDN_HEADS).astype(BF16)
    nw = nw_ref[...]
    units = [(b, hd) for b in range(batch) for hd in range(GDN_HEADS)]
    gbs = [gb_ref[b] for b in range(batch)]
    gcs = [sum(jnp.dot(tri, p, preferred_element_type=F32) for p in _split3(gb)) for gb in gbs]
    gcts = [sum(lax.dot_general(sel, p, (((1,), (1,)), ((), ())), preferred_element_type=F32)
                for p in _split3(gc)) for gc in gcs]

    qs, ks, gcols, egs, kbs = [], [], [], [], []
    for b, hd in units:
        sl = slice(hd * GDN_HEAD_DIM, (hd + 1) * GDN_HEAD_DIM)
        qs.append(q_ref[b, :, sl])
        ks.append(k_ref[b, :, sl])
        gcols.append(gcs[b][:, GDN_HEADS + hd:GDN_HEADS + hd + 1])
        egs.append(jnp.exp(gcols[-1]))
        kbs.append(ks[-1] * gbs[b][:, hd:hd + 1])
    kks = [_mm_nt(kb, kh) for kb, kh in zip(kbs, ks)]
    qks = [_mm_nt(qh, kh) for qh, kh in zip(qs, ks)]
    p_mats, sols = [], []
    for u, (b, hd) in enumerate(units):
        sl = slice(hd * GDN_HEAD_DIM, (hd + 1) * GDN_HEAD_DIM)
        decay = jnp.exp(jnp.where(causal, gcols[u] - gcts[b][hd:hd + 1, :], -1e30))
        p_mats.append(jnp.where(strict, -kks[u] * decay, 0.0))
        qks[u] = jnp.where(causal, qks[u] * decay, 0.0)
        sols.append(jnp.concatenate([v_ref[b, :, sl] * gbs[b][:, hd:hd + 1], kbs[u] * egs[u]], axis=1))
    n_fac = int(math.log2(CHUNK))
    for it in range(n_fac):
        sols = [sol + _mm_hi(p, sol) for p, sol in zip(p_mats, sols)]
        if it + 1 < n_fac:
            p_mats = [_mm_hi(p, p) for p in p_mats]
    sts = [state_scr[u] for u in range(len(units))]
    ws = [_mm(sol[:, GDN_HEAD_DIM:], st) for sol, st in zip(sols, sts)]
    os_ = [_mm(qh * eg, st) for qh, eg, st in zip(qs, egs, sts)]
    v_news = [sol[:, :GDN_HEAD_DIM] - w for sol, w in zip(sols, ws)]
    os_ = [o + _mm(qk, vn) for o, qk, vn in zip(os_, qks, v_news)]
    for u, (b, hd) in enumerate(units):
        sl = slice(hd * GDN_HEAD_DIM, (hd + 1) * GDN_HEAD_DIM)
        glast = gcols[u][CHUNK - 1:CHUNK, :]
        kd = ks[u] * jnp.exp(glast - gcols[u])
        state_scr[u] = sts[u] * jnp.exp(glast) + _mm_tn(kd, v_news[u])
        o = os_[u]
        on = o * lax.rsqrt(jnp.mean(o * o, axis=-1, keepdims=True) + EPS) * nw
        ya_ref[b, :, sl] = (on * _silu(z_ref[b, :, sl])).astype(ya_ref.dtype)


def _gdn(q, k, v, z, gb, norm_w, batch, seq):
    nc = seq // CHUNK
    t_rows = q.shape[0]
    blk = lambda w: pl.BlockSpec((batch, CHUNK, w), lambda c: (0, c, 0))
    r3 = lambda a: a.reshape(batch, seq, a.shape[-1])
    out = pl.pallas_call(
        functools.partial(_gdn_kernel, batch=batch),
        grid=(nc,),
        in_specs=[blk(GDN_WIDTH)] * 4 + [blk(LANES), pl.BlockSpec((1, GDN_HEAD_DIM), lambda c: (0, 0))],
        out_specs=blk(GDN_WIDTH),
        out_shape=jax.ShapeDtypeStruct((batch, seq, GDN_WIDTH), BF16),
        scratch_shapes=[pltpu.VMEM((batch * GDN_HEADS, GDN_HEAD_DIM, GDN_HEAD_DIM), F32)],
        compiler_params=_cparams("arbitrary"),
        name="gdn",
    )(r3(q), r3(k), r3(v), r3(z), r3(gb), norm_w)
    return out.reshape(t_rows, GDN_WIDTH)


def _s5_tables(lam_re, lam_im, log_step, b_re, b_im, c_re, c_im, n_lev):
    hp = lax.Precision.HIGHEST
    lr = jnp.minimum(lam_re.astype(F32), -1e-4)
    li = lam_im.astype(F32)
    dt = jnp.exp(log_step.astype(F32))[:, None]
    js = jnp.arange(S5_Q + 1, dtype=F32)[:, None, None]
    mag = jnp.exp(lr * dt * js)
    pr, pi = mag * jnp.cos(li * dt * js), mag * jnp.sin(li * dt * js)
    ar, ai = pr[1], pi[1]
    nr, ni = ar - 1.0, ai
    den = lr * lr + li * li
    cr, ci = (nr * lr + ni * li) / den, (ni * lr - nr * li) / den
    b_re, b_im = b_re.astype(F32), b_im.astype(F32)
    bbr = cr[..., None] * b_re - ci[..., None] * b_im
    bbi = cr[..., None] * b_im + ci[..., None] * b_re
    c_re, c_im = c_re.astype(F32), c_im.astype(F32)
    lbr = pr[..., None] * bbr - pi[..., None] * bbi
    lbi = pr[..., None] * bbi + pi[..., None] * bbr
    kern = (jnp.einsum('gon,jgni->gjoi', c_re, lbr[:S5_Q], precision=hp)
            - jnp.einsum('gon,jgni->gjoi', c_im, lbi[:S5_Q], precision=hp))
    s_idx = jnp.arange(S5_Q)[:, None]
    t_idx = jnp.arange(S5_Q)[None, :]
    lag = jnp.clip(t_idx - s_idx, 0, S5_Q - 1)
    toep = jnp.where((t_idx >= s_idx)[None, :, :, None, None], kern[:, lag], 0.0)
    toep = toep.transpose(0, 1, 4, 2, 3).reshape(S5_GROUPS, S5_Q * S5_GROUP, S5_Q * S5_GROUP)
    er = lbr[:S5_Q][::-1].transpose(1, 0, 3, 2).reshape(S5_GROUPS, S5_Q * S5_GROUP, S5_STATE)
    ei = lbi[:S5_Q][::-1].transpose(1, 0, 3, 2).reshape(S5_GROUPS, S5_Q * S5_GROUP, S5_STATE)
    e_mat = jnp.concatenate([er, ei], axis=-1)
    clr = c_re[None] * pr[1:, :, None, :] - c_im[None] * pi[1:, :, None, :]
    cli = c_re[None] * pi[1:, :, None, :] + c_im[None] * pr[1:, :, None, :]
    fr = clr.transpose(1, 3, 0, 2).reshape(S5_GROUPS, S5_STATE, S5_Q * S5_GROUP)
    fi = (-cli).transpose(1, 3, 0, 2).reshape(S5_GROUPS, S5_STATE, S5_Q * S5_GROUP)
    f_mat = jnp.concatenate([fr, fi], axis=1)
    a1, a2 = [], []
    cur_r, cur_i = pr[S5_Q], pi[S5_Q]
    for _ in range(n_lev):
        a1.append(jnp.concatenate([cur_r, cur_r], axis=-1))
        a2.append(jnp.concatenate([-cur_i, cur_i], axis=-1))
        cur_r, cur_i = cur_r * cur_r - cur_i * cur_i, 2.0 * cur_r * cur_i
    pad = [jnp.zeros_like(a1[0])] * (8 - n_lev % 8 if n_lev % 8 else 0)
    a1 = jnp.stack(a1 + pad, axis=1)
    a2 = jnp.stack(a2 + pad, axis=1)
    return toep.astype(BF16), e_mat.astype(BF16), f_mat.astype(BF16), a1, a2


def _s5_conv_kernel(u_ref, t_ref, e_ref, f_ref, a1_ref, a2_ref, y_ref, *, seg, n_lev):
    u = u_ref[0]
    y1 = jnp.dot(u, t_ref[0], preferred_element_type=F32)
    x = jnp.dot(u, e_ref[0], preferred_element_type=F32)
    rin = lax.broadcasted_iota(jnp.int32, x.shape, 0) % seg
    for lev in range(n_lev):
        s = 1 << lev
        xs = jnp.where(rin >= s, pltpu.roll(x, s, 0), 0.0)
        x = x + a1_ref[0, lev:lev + 1, :] * xs + a2_ref[0, lev:lev + 1, :] * pltpu.roll(xs, S5_STATE, 1)
    xp = jnp.where(rin >= 1, pltpu.roll(x, 1, 0), 0.0)
    y_ref[0] = y1 + _mm(xp, f_ref[0])


def _s5_conv(u_grp, toep, e_mat, f_mat, a1, a2, seg, n_lev):
    g, m, w = u_grp.shape
    kern = functools.partial(_s5_conv_kernel, seg=seg, n_lev=n_lev)
    per_g = lambda a: pl.BlockSpec((1,) + a.shape[1:], lambda i: (i, 0, 0))
    return pl.pallas_call(
        kern,
        grid=(g,),
        in_specs=[per_g(u_grp), per_g(toep), per_g(e_mat), per_g(f_mat), per_g(a1), per_g(a2)],
        out_specs=pl.BlockSpec((1, m, w), lambda i: (i, 0, 0)),
        out_shape=jax.ShapeDtypeStruct((g, m, w), F32),
        compiler_params=_cparams("parallel"),
        name="s5_conv",
    )(u_grp, toep, e_mat, f_mat, a1, a2)


def _s5_glu_kernel(y_ref, u_ref, d_ref, w_ref, b_ref, o_ref):
    y = y_ref[...] + d_ref[...] * u_ref[...]
    y = 0.5 * y * (1.0 + lax.erf(y * (2.0 ** -0.5)))
    ab = _mm(y, w_ref[...]) + b_ref[...]
    o_ref[...] = (ab[:, :S5_WIDTH] * jax.nn.sigmoid(ab[:, S5_WIDTH:])).astype(o_ref.dtype)


def _s5_glu(yconv, u, d_skip, glu_w, glu_b):
    t_rows = u.shape[0]
    tm = _row_tile(t_rows, 512)
    row = pl.BlockSpec((tm, S5_WIDTH), lambda i: (i, 0))
    full = lambda a: pl.BlockSpec(a.shape, lambda i: (0,) * a.ndim)
    return pl.pallas_call(
        _s5_glu_kernel,
        grid=(t_rows // tm,),
        in_specs=[row, row, full(d_skip), full(glu_w), full(glu_b)],
        out_specs=row,
        out_shape=jax.ShapeDtypeStruct((t_rows, S5_WIDTH), BF16),
        compiler_params=_cparams("parallel"),
        name="s5_glu",
    )(yconv, u, d_skip, glu_w, glu_b)


def _proj_res_kernel(*refs, n_in):
    x_ref, g_ref = refs[0], refs[1]
    ys = refs[2:2 + n_in]
    ws = refs[2 + n_in:2 + 2 * n_in]
    o_ref = refs[2 + 2 * n_in]
    acc = jnp.dot(ys[0][...], ws[0][...], preferred_element_type=F32)
    for y, w in zip(ys[1:], ws[1:]):
        acc = acc + jnp.dot(y[...], w[...], preferred_element_type=F32)
    o_ref[...] = x_ref[...] + g_ref[0] * acc


def _proj_res(x, gate, seq, ys, w):
    t_rows = x.shape[0]
    tm = _row_tile(seq, 512)
    n_in = len(ys)
    width = ys[0].shape[1]
    assert all(y.shape[1] == width for y in ys) and n_in * width == w.shape[0]
    return pl.pallas_call(
        functools.partial(_proj_res_kernel, n_in=n_in),
        grid=(t_rows // tm,),
        in_specs=([pl.BlockSpec((tm, D_MODEL), lambda i: (i, 0)), _mod_spec(tm, seq)]
                  + [pl.BlockSpec((tm, width), lambda i: (i, 0)) for _ in ys]
                  + [pl.BlockSpec((width, D_MODEL), lambda i, j=j: (j, 0)) for j in range(n_in)]),
        out_specs=pl.BlockSpec((tm, D_MODEL), lambda i: (i, 0)),
        out_shape=jax.ShapeDtypeStruct(x.shape, F32),
        compiler_params=_cparams("parallel"),
        name="proj_res",
    )(x, gate, *ys, *([w] * n_in))


def _proj_res_t_kernel(x_ref, g_ref, yt_ref, w_ref, o_ref):
    acc = lax.dot_general(yt_ref[...], w_ref[...], (((0,), (0,)), ((), ())), preferred_element_type=F32)
    o_ref[...] = x_ref[...] + g_ref[0] * acc


def _proj_res_t(x, gate, seq, yt, w):
    t_rows = x.shape[0]
    tm = _row_tile(seq, 512)
    row = pl.BlockSpec((tm, D_MODEL), lambda i: (i, 0))
    return pl.pallas_call(
        _proj_res_t_kernel,
        grid=(t_rows // tm,),
        in_specs=[row, _mod_spec(tm, seq), pl.BlockSpec((yt.shape[0], tm), lambda i: (0, i)),
                  pl.BlockSpec(w.shape, lambda i: (0, 0))],
        out_specs=row,
        out_shape=jax.ShapeDtypeStruct(x.shape, F32),
        compiler_params=_cparams("parallel"),
        name="proj_res_t",
    )(x, gate, yt, w)


def _ffn_kernel(x_ref, nw_ref, sc_ref, sh_ref, g_ref, w1_ref, w3_ref, w2_ref, o_ref):
    x = x_ref[...]
    h = _norm_mod(x, nw_ref[...], sc_ref[0], sh_ref[0]).astype(BF16)
    a = jnp.dot(h, w1_ref[...], preferred_element_type=F32)
    b = jnp.dot(h, w3_ref[...], preferred_element_type=F32)
    f = jnp.dot((_silu(a) * b).astype(BF16), w2_ref[...], preferred_element_type=F32)
    o_ref[...] = x + g_ref[0] * f


def _ffn(x, seq, nw, sc, sh, gate, w13, w2):
    t_rows = x.shape[0]
    tm = _row_tile(seq, 256)
    row = pl.BlockSpec((tm, D_MODEL), lambda i: (i, 0))
    once = lambda a: pl.BlockSpec(a.shape, lambda i: (0,) * a.ndim, pipeline_mode=pl.Buffered(1))
    half = lambda j: pl.BlockSpec((D_MODEL, D_FF), lambda i: (0, j), pipeline_mode=pl.Buffered(1))
    return pl.pallas_call(
        _ffn_kernel,
        grid=(t_rows // tm,),
        in_specs=[row, once(nw), _mod_spec(tm, seq), _mod_spec(tm, seq), _mod_spec(tm, seq),
                  half(0), half(1), once(w2)],
        out_specs=row,
        out_shape=jax.ShapeDtypeStruct(x.shape, F32),
        compiler_params=_cparams("parallel"),
        name="ffn",
    )(x, nw, sc, sh, gate, w13, w13, w2)


LOG2E = 1.4426950408889634


def _odd_qkv_kernel(x_ref, nw_ref, sc_ref, sh_ref, w_ref, wvt_ref, qw_ref, kw_ref, q_ref, k_ref, vt_ref):
    h = _norm_mod(x_ref[...], nw_ref[...], sc_ref[0], sh_ref[0]).astype(BF16)
    proj = jnp.dot(h, w_ref[...], preferred_element_type=F32)
    r = lax.broadcasted_iota(jnp.int32, (LANES, LANES), 0) // DIFF_HEAD_DIM
    c = lax.broadcasted_iota(jnp.int32, (LANES, LANES), 1) // DIFF_HEAD_DIM
    grp = (r == c).astype(BF16)
    q_scale = DIFF_HEAD_DIM ** -0.5 * LOG2E
    for base, w_vec, scale, out in ((0, qw_ref, q_scale, q_ref), (D_MODEL, kw_ref, 1.0, k_ref)):
        for t in range(D_MODEL // LANES):
            xt = proj[:, base + t * LANES:base + (t + 1) * LANES]
            ss = _mm(xt * xt, grp)
            y = xt * lax.rsqrt(ss * (1.0 / DIFF_HEAD_DIM) + EPS) * w_vec[...]
            out[:, t * LANES:(t + 1) * LANES] = (y * scale).astype(out.dtype)
    vt_ref[...] = _mm_nt(wvt_ref[...], h).astype(vt_ref.dtype)


def _odd_qkv(x, seq, nw, sc, sh, w_qk, w_vt, qw, kw):
    t_rows = x.shape[0]
    tm = _row_tile(seq, 512)
    row = pl.BlockSpec((tm, D_MODEL), lambda i: (i, 0))
    full = lambda a: pl.BlockSpec(a.shape, lambda i: (0,) * a.ndim)
    out = jax.ShapeDtypeStruct((t_rows, D_MODEL), BF16)
    return pl.pallas_call(
        _odd_qkv_kernel,
        grid=(t_rows // tm,),
        in_specs=[row, full(nw), _mod_spec(tm, seq), _mod_spec(tm, seq),
                  pl.BlockSpec((D_MODEL, 2 * D_MODEL), lambda i: (0, 0)), full(w_vt), full(qw),
                  full(kw)],
        out_specs=[row, row, pl.BlockSpec((D_MODEL, tm), lambda i: (0, i))],
        out_shape=[out, out, jax.ShapeDtypeStruct((D_MODEL, t_rows), BF16)],
        compiler_params=_cparams("parallel"),
        name="odd_qkv",
    )(x, nw, sc, sh, w_qk, w_vt, qw, kw)


def _attn_kernel(bound_ref, q_ref, k_ref, vt_ref, lam_ref, sw_ref, o_ref, acc_scr, pp_scr, l_scr,
                 *, tq, tk, heads, out_scale):
    i = pl.program_id(2)
    lane = lax.broadcasted_iota(jnp.int32, (tq, LANES), 1)
    qs = []
    for hh in range(heads):
        q = q_ref[:, hh * LANES:(hh + 1) * LANES]
        zero = jnp.zeros_like(q)
        qs += [jnp.where(lane < DIFF_HEAD_DIM, q, zero), jnp.where(lane >= DIFF_HEAD_DIM, q, zero)]
    n_sub = tq // LANES
    n_full = i * (tq // tk)

    def load_k(j, hh):
        k0 = pl.multiple_of(j * tk, tk)
        return k0, k_ref[pl.ds(k0, tk), hh * LANES:(hh + 1) * LANES]

    def load_v(j, hh):
        return vt_ref[hh * LANES:(hh + 1) * LANES, pl.ds(pl.multiple_of(j * tk, tk), tk)]

    def scores(kb, u, c, k0, masked):
        s = lax.dot_general(kb, qs[u][c * LANES:(c + 1) * LANES, :], (((1,), (1,)), ((), ())),
                            preferred_element_type=F32)
        if masked:
            kc = (k0 + lax.broadcasted_iota(jnp.int32, (tk, LANES), 0)) // CHUNK
            qc = (i * tq + c * LANES + lax.broadcasted_iota(jnp.int32, (tk, LANES), 1)) // CHUNK
            s = jnp.where(kc <= qc, s, -1e30)
        return s

    def finish(ls):
        lam = lam_ref[0:1, 0:1]
        for hh in range(heads):
            o = acc_scr[2 * hh] / ls[2 * hh] - lam * (acc_scr[2 * hh + 1] / ls[2 * hh + 1])
            o = o * lax.rsqrt(jnp.mean(o * o, axis=0, keepdims=True) + EPS) * sw_ref[...] * out_scale
            o_ref[hh * LANES:(hh + 1) * LANES, :] = o.astype(o_ref.dtype)

    acc_scr[...] = jnp.zeros_like(acc_scr)

    @pl.when(bound_ref[0] == 1)
    def _():
        l_scr[...] = jnp.zeros_like(l_scr)
        pp_scr[1] = jnp.zeros(pp_scr.shape[1:], BF16)

        def values(j, hh):
            vb = load_v(j, hh)
            for u in (2 * hh, 2 * hh + 1):
                acc_scr[u] = acc_scr[u] + jnp.dot(vb, pp_scr[j % 2, u], preferred_element_type=F32)

        def visibility(jj, c):
            k_lo, k_hi = (jj * tk) // CHUNK, ((jj + 1) * tk - 1) // CHUNK
            q_lo, q_hi = (c * LANES) // CHUNK, ((c + 1) * LANES - 1) // CHUNK
            return "full" if k_hi <= q_lo else ("none" if k_lo > q_hi else "part")

        def step(j, jj):
            jp = jnp.maximum(j - 1, 0) + (j == 0).astype(jnp.int32)
            for hh in range(heads):
                values(jp, hh)
                k0, kb = load_k(j, hh)
                for u in (2 * hh, 2 * hh + 1):
                    for c in range(n_sub):
                        cs = slice(c * LANES, (c + 1) * LANES)
                        vis = "full" if jj is None else visibility(jj, c)
                        if vis == "none":
                            pp_scr[j % 2, u, :, cs] = jnp.zeros((tk, LANES), BF16)
                            continue
                        p = jnp.exp2(scores(kb, u, c, k0, vis == "part"))
                        l_scr[u, :, cs] = l_scr[u, :, cs] + jnp.sum(p.reshape(tk // 8, 8, LANES), axis=0)
                        pp_scr[j % 2, u, :, cs] = p.astype(BF16)

        def body(j, carry):
            step(j, None)
            return carry

        lax.fori_loop(0, n_full, body, 0)
        n_mask = tq // tk
        for jj in range(n_mask):
            step(n_full + jj, jj)
        for hh in range(heads):
            values(n_full + n_mask - 1, hh)
        finish([jnp.sum(l_scr[u], axis=0, keepdims=True) for u in range(2 * heads)])

    @pl.when(bound_ref[0] != 1)
    def _():
        neg = jnp.full((1, tq), -1e30, F32)
        zer = jnp.zeros((1, tq), F32)

        def block(j, carry, masked):
            out = []
            for hh in range(heads):
                k0, kb = load_k(j, hh)
                vb = load_v(j, hh)
                alphas = []
                for u in (2 * hh, 2 * hh + 1):
                    m, l = carry[2 * u], carry[2 * u + 1]
                    m_parts, l_parts, a_parts = [], [], []
                    for c in range(n_sub):
                        cs = slice(c * LANES, (c + 1) * LANES)
                        s = scores(kb, u, c, k0, masked)
                        m_new = jnp.maximum(m[:, cs], jnp.max(s, axis=0, keepdims=True))
                        p = jnp.exp2(s - m_new)
                        alpha = jnp.exp2(m[:, cs] - m_new)
                        l_parts.append(alpha * l[:, cs] + jnp.sum(p, axis=0, keepdims=True))
                        pp_scr[0, u, :, cs] = p.astype(BF16)
                        m_parts.append(m_new)
                        a_parts.append(alpha)
                    out += [jnp.concatenate(m_parts, axis=1), jnp.concatenate(l_parts, axis=1)]
                    alphas.append(jnp.concatenate(a_parts, axis=1))
                for t, u in enumerate((2 * hh, 2 * hh + 1)):
                    acc_scr[u] = alphas[t] * acc_scr[u] + jnp.dot(vb, pp_scr[0, u], preferred_element_type=F32)
            return tuple(out)

        carry = lax.fori_loop(0, n_full, lambda j, c: block(j, c, False), (neg, zer) * (2 * heads))
        for jj in range(tq // tk):
            carry = block(n_full + jj, carry, True)
        finish([carry[2 * u + 1] for u in range(2 * heads)])


ATTN_HEADS_PER_STEP = 4
SCORE_BOUND_LOG2 = 64.0


def _attention(q, k, vt, score_bound, lam_vec, subln_col, batch, seq, out_scale):
    tq = _row_tile(seq, 512)
    tk = _row_tile(tq, 256)
    nq = seq // tq
    assert seq // tk >= 2
    hp = ATTN_HEADS_PER_STEP
    wid = hp * LANES
    kern = functools.partial(_attn_kernel, tq=tq, tk=tk, heads=hp, out_scale=out_scale)
    full = lambda a: pl.BlockSpec(a.shape, lambda b, h, i: (0,) * a.ndim)
    bounded = (score_bound <= SCORE_BOUND_LOG2).astype(jnp.int32).reshape(1)
    return pl.pallas_call(
        kern,
        grid=(batch, DIFF_HEADS // hp, nq),
        in_specs=[pl.BlockSpec(memory_space=pltpu.SMEM),
                  pl.BlockSpec((tq, wid), lambda b, h, i: (b * nq + i, h)),
                  pl.BlockSpec((seq, wid), lambda b, h, i: (b, h)),
                  pl.BlockSpec((wid, seq), lambda b, h, i: (h, b)),
                  full(lam_vec), full(subln_col)],
        out_specs=pl.BlockSpec((wid, tq), lambda b, h, i: (h, b * nq + i)),
        out_shape=jax.ShapeDtypeStruct(vt.shape, BF16),
        scratch_shapes=[pltpu.VMEM((2 * hp, LANES, tq), F32),
                        pltpu.VMEM((2, 2 * hp, tk, tq), BF16), pltpu.VMEM((2 * hp, 8, tq), F32)],
        compiler_params=_cparams("parallel", "parallel", "arbitrary"),
        name="diff_attn",
    )(bounded, q, k, vt, lam_vec, subln_col)


def _router_kernel(x_ref, nw_ref, sc_ref, sh_ref, whi_ref, wlo_ref, h_ref, idx_ref, gate_ref):
    h = _norm_mod(x_ref[...], nw_ref[...], sc_ref[0], sh_ref[0])
    h_ref[...] = h
    h_hi = h.astype(BF16)
    h_lo = (h - h_hi.astype(F32)).astype(BF16)
    logits = (jnp.dot(h_hi, whi_ref[...], preferred_element_type=F32)
              + jnp.dot(h_hi, wlo_ref[...], preferred_element_type=F32)
              + jnp.dot(h_lo, whi_ref[...], preferred_element_type=F32))
    lane = lax.broadcasted_iota(jnp.int32, logits.shape, 1)
    lane_f = lane.astype(F32)
    logits = jnp.where(lane < N_EXPERTS, logits, -jnp.inf)
    m1 = jnp.max(logits, axis=-1, keepdims=True)
    i1 = jnp.min(jnp.where(logits == m1, lane_f, float(LANES)), axis=-1, keepdims=True)
    rest = jnp.where(lane_f == i1, -jnp.inf, logits)
    m2 = jnp.max(rest, axis=-1, keepdims=True)
    i2 = jnp.min(jnp.where(rest == m2, lane_f, float(LANES)), axis=-1, keepdims=True)
    e = jnp.exp(m2 - m1)
    g1 = 1.0 / (1.0 + e)
    idx_ref[...] = jnp.where(lane == 0, i1, jnp.where(lane == 1, i2, 0.0)).astype(jnp.int32)
    gate_ref[...] = jnp.where(lane == 0, g1, jnp.where(lane == 1, e * g1, 0.0))


def _router(x, seq, nw, sc, sh, w_hi, w_lo):
    t_rows = x.shape[0]
    tm = _row_tile(seq, 512)
    row = lambda w: pl.BlockSpec((tm, w), lambda i: (i, 0))
    full = lambda a: pl.BlockSpec(a.shape, lambda i: (0,) * a.ndim)
    return pl.pallas_call(
        _router_kernel,
        grid=(t_rows // tm,),
        in_specs=[row(D_MODEL), full(nw), _mod_spec(tm, seq), _mod_spec(tm, seq), full(w_hi), full(w_lo)],
        out_specs=[row(D_MODEL), row(LANES), row(LANES)],
        out_shape=[jax.ShapeDtypeStruct((t_rows, D_MODEL), F32),
                   jax.ShapeDtypeStruct((t_rows, LANES), jnp.int32),
                   jax.ShapeDtypeStruct((t_rows, LANES), F32)],
        compiler_params=_cparams("parallel"),
        name="router",
    )(x, nw, sc, sh, w_hi, w_lo)


def _start_row_gather(idx_ref, src_hbm, dst_ref, sem, n_rows, inline=False):
    def start(r2, _):
        for prio in range(2):
            r = 2 * r2 + prio
            pltpu.make_async_copy(src_hbm.at[pl.ds(idx_ref[0, 0, r], 1), :], dst_ref.at[pl.ds(r, 1), :],
                                  sem).start(priority=prio)
        return 0

    if inline:
        for r2 in range(n_rows // 2):
            start(r2, 0)
    else:
        lax.fori_loop(0, n_rows // 2, start, 0, unroll=4)


def _wait_row_gather(src_hbm, dst_ref, sem, n_rows):
    pltpu.make_async_copy(src_hbm.at[pl.ds(0, n_rows), :], dst_ref, sem).wait()


def _prefetched_rows(i, idx_cur_ref, idx_next_ref, src_hbm, buf, sems, n_rows):
    slot = i % 2

    @pl.when(i == 0)
    def _():
        _start_row_gather(idx_cur_ref, src_hbm, buf.at[0], sems.at[0], n_rows)

    @pl.when(i + 1 < pl.num_programs(0))
    def _():
        _start_row_gather(idx_next_ref, src_hbm, buf.at[1 - slot], sems.at[1 - slot], n_rows)

    _wait_row_gather(src_hbm, buf.at[slot], sems.at[slot], n_rows)
    return buf.at[slot]


def _expert_kernel(te_ref, idx_cur_ref, idx_next_ref, h_hbm, w1_ref, w3_ref, w2_ref, o_ref, xbuf, sems):
    i = pl.program_id(0)
    last = pl.num_programs(0) - 1
    slot = i % 2

    @pl.when(i == 0)
    def _():
        _start_row_gather(idx_cur_ref, h_hbm, xbuf.at[0], sems.at[0], EXPERT_TM)

    _wait_row_gather(h_hbm, xbuf.at[slot], sems.at[slot], EXPERT_TM)
    _start_row_gather(idx_next_ref, h_hbm, xbuf.at[1 - slot], sems.at[1 - slot], EXPERT_TM, inline=True)
    h = xbuf[slot].astype(BF16)
    a = jnp.dot(h, w1_ref[0], preferred_element_type=F32)
    b = jnp.dot(h, w3_ref[0], preferred_element_type=F32)
    o_ref[...] = jnp.dot((_silu(a) * b).astype(BF16), w2_ref[0], preferred_element_type=F32)

    @pl.when(i == last)
    def _():
        _wait_row_gather(h_hbm, xbuf.at[1 - slot], sems.at[1 - slot], EXPERT_TM)


EXPERT_TM = 256


def _expert_ffn(h, tok_of_slot, tile_expert, w13, w2):
    tm = EXPERT_TM
    n_tiles = tok_of_slot.shape[0] // tm
    idx = tok_of_slot.reshape(n_tiles, 1, tm)
    return pl.pallas_call(
        _expert_kernel,
        grid_spec=pltpu.PrefetchScalarGridSpec(
            num_scalar_prefetch=1,
            grid=(n_tiles,),
            in_specs=[pl.BlockSpec((1, 1, tm), lambda i, te: (i, 0, 0), memory_space=pltpu.SMEM),
                      pl.BlockSpec((1, 1, tm), lambda i, te: (jnp.minimum(i + 1, n_tiles - 1), 0, 0),
                                   memory_space=pltpu.SMEM),
                      pl.BlockSpec(memory_space=pl.ANY),
                      pl.BlockSpec((1, D_MODEL, D_FF), lambda i, te: (te[i], 0, 0)),
                      pl.BlockSpec((1, D_MODEL, D_FF), lambda i, te: (te[i], 0, 1)),
                      pl.BlockSpec((1, D_FF, D_MODEL), lambda i, te: (te[i], 0, 0))],
            out_specs=pl.BlockSpec((tm, D_MODEL), lambda i, te: (i, 0)),
            scratch_shapes=[pltpu.VMEM((2, tm, D_MODEL), F32), pltpu.SemaphoreType.DMA((2,))],
        ),
        out_shape=jax.ShapeDtypeStruct((n_tiles * tm, D_MODEL), F32),
        compiler_params=_cparams("arbitrary"),
        name="expert_ffn",
    )(tile_expert, idx, idx, h, w13, w13, w2)


COMBINE_TM = 256


def _combine_kernel(p0_cur, p0_next, p1_cur, p1_next, x_ref, g_ref, gate_ref, y_hbm, o_ref, buf0, buf1, sems0,
                    sems1):
    i = pl.program_id(0)
    y0 = _prefetched_rows(i, p0_cur, p0_next, y_hbm, buf0, sems0, COMBINE_TM)
    y1 = _prefetched_rows(i, p1_cur, p1_next, y_hbm, buf1, sems1, COMBINE_TM)
    gates = gate_ref[...]
    f = gates[:, 0:1] * y0[...] + gates[:, 1:2] * y1[...]
    o_ref[...] = x_ref[...] + g_ref[0] * f


def _combine(x, gate_mod, seq, gates, y_perm, pos):
    t_rows = x.shape[0]
    tm = COMBINE_TM
    nt = t_rows // tm
    row = pl.BlockSpec((tm, D_MODEL), lambda i: (i, 0))
    cur = pl.BlockSpec((1, 1, tm), lambda i: (i, 0, 0), memory_space=pltpu.SMEM)
    nxt = pl.BlockSpec((1, 1, tm), lambda i: (jnp.minimum(i + 1, nt - 1), 0, 0), memory_space=pltpu.SMEM)
    p0 = pos[:, 0].reshape(nt, 1, tm)
    p1 = pos[:, 1].reshape(nt, 1, tm)
    return pl.pallas_call(
        _combine_kernel,
        grid=(nt,),
        in_specs=[cur, nxt, cur, nxt, row, _mod_spec(tm, seq), pl.BlockSpec((tm, LANES), lambda i: (i, 0)),
                  pl.BlockSpec(memory_space=pl.ANY)],
        out_specs=row,
        out_shape=jax.ShapeDtypeStruct(x.shape, F32),
        scratch_shapes=[pltpu.VMEM((2, tm, D_MODEL), F32), pltpu.VMEM((2, tm, D_MODEL), F32),
                        pltpu.SemaphoreType.DMA((2,)), pltpu.SemaphoreType.DMA((2,))],
        compiler_params=_cparams("arbitrary"),
        name="moe_combine",
    )(p0, p0, p1, p1, x, gate_mod, gates, y_perm)


def _dispatch_plan(expert_idx, t_rows):
    tm = EXPERT_TM
    flat = expert_idx.reshape(-1)
    onehot = (flat[:, None] == jnp.arange(N_EXPERTS)[None, :]).astype(jnp.int32)
    rank = jnp.take_along_axis(jnp.cumsum(onehot, axis=0) - onehot, flat[:, None], axis=1)[:, 0]
    counts = jnp.sum(onehot, axis=0)
    tiles_per = (counts + tm - 1) // tm
    tile_end = jnp.cumsum(tiles_per)
    offs = (tile_end - tiles_per) * tm
    pos = offs[flat] + rank
    n_tiles = (2 * t_rows) // tm + N_EXPERTS
    p_rows = n_tiles * tm
    tok_of_slot = jnp.zeros((p_rows,), jnp.int32).at[pos].set(jnp.arange(2 * t_rows, dtype=jnp.int32) // 2)
    tile_expert = jnp.minimum(
        jnp.sum(jnp.arange(n_tiles, dtype=jnp.int32)[:, None] >= tile_end[None, :], axis=1), N_EXPERTS - 1
    ).astype(jnp.int32)
    return tok_of_slot, tile_expert, pos.reshape(t_rows, 2).astype(jnp.int32)


def _mods(mod, layer, batch):
    m = mod[layer, :batch].reshape(batch, 6, 1, D_MODEL)
    return [m[:, j] for j in range(6)]


def _even_layer(x, batch, seq, mods, nw_mix, nw_ffn, w_in, conv_w, a_log, dt_bias, gdn_norm_w,
                lam_re, lam_im, log_step, b_re, b_im, c_re, c_im, d_skip, glu_w, glu_b, w_out, w13, w2):
    sh1, sc1, g1, sh2, sc2, g2 = mods
    t_rows = batch * seq
    qkvz, rest = w_in[:, :4 * GDN_WIDTH], w_in[:, 4 * GDN_WIDTH:]
    w_ba, w_u = rest[:, :2 * GDN_HEADS], rest[:, 2 * GDN_HEADS:]
    w_cat = jnp.concatenate(
        [qkvz, w_u, w_ba, jnp.zeros((D_MODEL, LANES - 2 * GDN_HEADS), w_in.dtype)], axis=1).astype(BF16)
    gate_params = jnp.zeros((8, LANES), F32)
    gate_params = gate_params.at[0, GDN_HEADS:2 * GDN_HEADS].set(a_log.astype(F32))
    gate_params = gate_params.at[1, GDN_HEADS:2 * GDN_HEADS].set(dt_bias.astype(F32))
    q, k, v, z, u, gb = _even_in(x, seq, nw_mix, sc1, sh1, w_cat, conv_w.astype(F32), gate_params)
    y_a = _gdn(q, k, v, z, gb, gdn_norm_w.reshape(1, GDN_HEAD_DIM).astype(F32), batch, seq)

    seg = seq // S5_Q
    n_lev = max(int(math.ceil(math.log2(seg))), 0)
    toep, e_mat, f_mat, a1, a2 = _s5_tables(lam_re, lam_im, log_step, b_re, b_im, c_re, c_im, max(n_lev, 1))
    m_rows = t_rows // S5_Q
    u_grp = (u.reshape(m_rows, S5_Q, S5_GROUPS, S5_GROUP).transpose(2, 0, 1, 3)
             .reshape(S5_GROUPS, m_rows, S5_Q * S5_GROUP).astype(BF16))
    yg = _s5_conv(u_grp, toep, e_mat, f_mat, a1, a2, seg, n_lev)
    yconv = (yg.reshape(S5_GROUPS, m_rows, S5_Q, S5_GROUP).transpose(1, 2, 0, 3)
             .reshape(t_rows, S5_WIDTH))
    y_b = _s5_glu(yconv, u, d_skip.reshape(1, S5_WIDTH).astype(F32), glu_w.astype(BF16),
                  glu_b.reshape(1, 2 * S5_WIDTH).astype(F32))
    wo = w_out.astype(BF16)
    x = _proj_res(x, g1, seq, [y_a, y_b], wo)
    return _ffn(x, seq, nw_ffn, sc2, sh2, g2, _to_bf16(w13), _to_bf16(w2))


def _odd_layer(x, batch, seq, mods, nw_mix, nw_ffn, w_qkv, q_norm_w, k_norm_w, lq1, lk1, lq2, lk2,
               subln_w, w_out, router_w, w13, w2, lambda_init):
    sh1, sc1, g1, sh2, sc2, g2 = mods
    t_rows = batch * seq
    tile2 = lambda w: jnp.tile(w.astype(F32), LANES // DIFF_HEAD_DIM).reshape(1, LANES)
    w_qkv_b = _to_bf16(w_qkv)
    q, k, vt = _odd_qkv(x, seq, nw_mix, sc1, sh1, w_qkv_b, w_qkv_b[:, 2 * D_MODEL:].T,
                        tile2(q_norm_w), tile2(k_norm_w))
    lam = (jnp.exp(jnp.sum(lq1.astype(F32) * lk1.astype(F32)))
           - jnp.exp(jnp.sum(lq2.astype(F32) * lk2.astype(F32))) + lambda_init)
    lam_vec = jnp.full((8, LANES), lam, F32)
    score_bound = (1.02 * LOG2E * DIFF_HEAD_DIM ** 0.5
                   * jnp.max(jnp.abs(q_norm_w.astype(F32))) * jnp.max(jnp.abs(k_norm_w.astype(F32))))
    ot = _attention(q, k, vt, score_bound, lam_vec, subln_w.reshape(LANES, 1).astype(F32), batch, seq,
                    1.0 - lambda_init)
    x = _proj_res_t(x, g1, seq, ot, w_out.astype(BF16))

    rw = jnp.zeros((D_MODEL, LANES), F32).at[:, :N_EXPERTS].set(router_w.astype(F32))
    rw_hi = rw.astype(BF16)
    rw_lo = (rw - rw_hi.astype(F32)).astype(BF16)
    h, idx, gates = _router(x, seq, nw_ffn, sc2, sh2, rw_hi, rw_lo)
    tok_of_slot, tile_expert, pos = _dispatch_plan(idx[:, :2], t_rows)
    y_perm = _expert_ffn(h, tok_of_slot, tile_expert, _to_bf16(w13), _to_bf16(w2))
    return _combine(x, g2, seq, gates, y_perm, pos)


def kernel(x, c, ada_w, ada_b, norm_mix_w, norm_ffn_w, even_w_in, even_conv_w, even_a_log, even_dt_bias, even_gdn_norm_w, even_lam_re, even_lam_im, even_log_step, even_b_re, even_b_im, even_c_re, even_c_im, even_d_skip, even_glu_w, even_glu_b, even_w_out, even_ffn_w13, even_ffn_w2, odd_w_qkv, odd_q_norm_w, odd_k_norm_w, odd_lambda_q1, odd_lambda_k1, odd_lambda_q2, odd_lambda_k2, odd_subln_w, odd_w_out, odd_router_w, odd_expert_w13, odd_expert_w2):
    batch, seq, d = x.shape
    assert d == D_MODEL and seq % CHUNK == 0 and seq % S5_Q == 0
    c_pad = jnp.zeros((8, d), F32).at[:batch].set(c.astype(F32))
    mod = _adaln(c_pad, ada_w, ada_b.astype(F32))
    xf = x.astype(F32).reshape(batch * seq, d)
    for layer in range(DEPTH):
        i = layer // 2
        mods = _mods(mod, layer, batch)
        nw_mix = norm_mix_w[layer].reshape(1, d).astype(F32)
        nw_ffn = norm_ffn_w[layer].reshape(1, d).astype(F32)
        if layer % 2 == 0:
            xf = _even_layer(xf, batch, seq, mods, nw_mix, nw_ffn, even_w_in[i], even_conv_w[i], even_a_log[i],
                             even_dt_bias[i], even_gdn_norm_w[i], even_lam_re[i], even_lam_im[i],
                             even_log_step[i], even_b_re[i], even_b_im[i], even_c_re[i], even_c_im[i],
                             even_d_skip[i], even_glu_w[i], even_glu_b[i], even_w_out[i],
                             even_ffn_w13[i], even_ffn_w2[i])
        else:
            lambda_init = 0.8 - 0.6 * math.exp(-0.3 * layer)
            xf = _odd_layer(xf, batch, seq, mods, nw_mix, nw_ffn, odd_w_qkv[i], odd_q_norm_w[i], odd_k_norm_w[i],
                            odd_lambda_q1[i], odd_lambda_k1[i], odd_lambda_q2[i], odd_lambda_k2[i],
                            odd_subln_w[i], odd_w_out[i], odd_router_w[i], odd_expert_w13[i],
                            odd_expert_w2[i], lambda_init)
    return xf.reshape(batch, seq, d).astype(x.dtype)
```
